```python
import math
import jax, jax.numpy as jnp
from jax import lax
import numpy as np

D_MODEL = 1024
BATCH = 16
SEQ = 4096
DEPTH = 4
DEC_BATCH = 16
DEC_SEQ = 2048
PAST_LEN = 128

D_MIX = D_MODEL
D_ATTN = D_MIX // 2
D_HY = D_MIX - D_ATTN
HEAD_DIM = 64
N_Q_HEADS = D_ATTN // HEAD_DIM
N_KV_HEADS = 2
GQA_GROUP = N_Q_HEADS // N_KV_HEADS
KV_WIDTH = N_KV_HEADS * HEAD_DIM
ROT_DIM = HEAD_DIM // 4
ROPE_THETA = 500000.0
WINDOW = 128
BLOCK = 128
HY_ORDER = 2
SHORT_K = 3
FILTER_EMB = 33
FILTER_HID = 64
DECAY_TARGET = 1e-2
FAST_DECAY_PCT = 0.3
SLOW_DECAY_PCT = 1.5
D_FF = 2816
N_IN = D_ATTN + 2 * KV_WIDTH + (HY_ORDER + 1) * D_HY
N_MOD = 9
ALPHA = float((2 * DEPTH) ** 0.25)
BETA = float((8 * DEPTH) ** -0.25)
LN_EPS = 1e-5
RMS_EPS = 1e-6

kernel_name = "hymba_attn_hyena_macaron_deepnorm_encoder"


def _layer_norm(x, g, b):
    xf = x.astype(jnp.float32)
    mu = xf.mean(-1, keepdims=True)
    var = jnp.square(xf - mu).mean(-1, keepdims=True)
    return ((xf - mu) * lax.rsqrt(var + LN_EPS) * g.astype(jnp.float32) + b.astype(jnp.float32)).astype(x.dtype)


def _rms_norm(x, g):
    xf = x.astype(jnp.float32)
    ms = jnp.square(xf).mean(-1, keepdims=True)
    return (xf * lax.rsqrt(ms + RMS_EPS) * g.astype(jnp.float32)).astype(x.dtype)


def _swiglu(h, wi, wo):
    g, u = jnp.split(h @ wi, 2, axis=-1)
    return (jax.nn.silu(g) * u) @ wo


def _partial_rope(x, L):
    inv = ROPE_THETA ** (-jnp.arange(0, ROT_DIM, 2, dtype=jnp.float32) / ROT_DIM)
    ang = jnp.arange(L, dtype=jnp.float32)[:, None] * inv[None]
    cos = jnp.cos(ang)[None, :, None, :]
    sin = jnp.sin(ang)[None, :, None, :]
    xr = x[..., :ROT_DIM].astype(jnp.float32)
    x1, x2 = xr[..., : ROT_DIM // 2], xr[..., ROT_DIM // 2:]
    rot = jnp.concatenate([x1 * cos - x2 * sin, x2 * cos + x1 * sin], axis=-1)
    return jnp.concatenate([rot.astype(x.dtype), x[..., ROT_DIM:]], axis=-1)


def _window_attention(q, k, v, sink):
    B, L = q.shape[0], q.shape[1]
    nb = L // BLOCK
    pad = ((0, 0), (BLOCK, BLOCK), (0, 0), (0, 0))

    def bands(t):
        tb = jnp.pad(t, pad).reshape(B, nb + 2, BLOCK, N_KV_HEADS, HEAD_DIM)
        return jnp.concatenate([tb[:, :-2], tb[:, 1:-1], tb[:, 2:]], axis=2)

    kb, vb = bands(k), bands(v)
    qb = q.reshape(B, nb, BLOCK, N_KV_HEADS, GQA_GROUP, HEAD_DIM)
    s = jnp.einsum('bnqkgd,bnskd->bnkgqs', qb, kb, preferred_element_type=jnp.float32) * (HEAD_DIM ** -0.5)
    qpos = jnp.arange(nb)[:, None] * BLOCK + jnp.arange(BLOCK)[None]
    kpos = (jnp.arange(nb)[:, None] - 1) * BLOCK + jnp.arange(3 * BLOCK)[None]
    valid = ((jnp.abs(qpos[:, :, None] - kpos[:, None, :]) <= WINDOW)
             & (kpos[:, None, :] >= 0) & (kpos[:, None, :] < L))
    s = jnp.where(valid[None, :, None, None], s, -jnp.inf)
    sink_f = sink.astype(jnp.float32).reshape(N_KV_HEADS, GQA_GROUP)[None, None, :, :, None, None]
    m = jnp.maximum(s.max(-1, keepdims=True), sink_f)
    p = jnp.exp(s - m)
    p = p / (p.sum(-1, keepdims=True) + jnp.exp(sink_f - m))
    o = jnp.einsum('bnkgqs,bnskd->bnqkgd', p.astype(v.dtype), vb)
    return o.reshape(B, L, N_Q_HEADS * HEAD_DIM)


def _short_conv(u, w, b):
    up = jnp.pad(u, ((0, 0), (1, 1), (0, 0)))
    return up[:, :-2] * w[0] + up[:, 1:-1] * w[1] + up[:, 2:] * w[2] + b


def _hyena_filters(L, w1, b1, w2, b2, w3, b3, w4, freq, decay):
    f32 = jnp.float32
    t = jnp.linspace(0.0, 1.0, L, dtype=f32)[:, None]
    n_bands = (FILTER_EMB - 1) // 2
    w = 2.0 * math.pi * jnp.arange(L, dtype=f32)[:, None] / L
    fb = jnp.linspace(1e-4, n_bands - 1, n_bands, dtype=f32)[None]
    z = jnp.concatenate([t, jnp.cos(fb * w), -jnp.sin(fb * w)], axis=-1)
    fr = freq.astype(f32)
    h = jnp.sin(fr * (z @ w1.astype(f32) + b1.astype(f32)))
    h = jnp.sin(fr * (h @ w2.astype(f32) + b2.astype(f32)))
    h = jnp.sin(fr * (h @ w3.astype(f32) + b3.astype(f32)))
    h = (h @ w4.astype(f32)).reshape(L, 2, D_HY)
    window = jnp.exp(-t[:, :, None] * jnp.abs(decay.astype(f32))[None])
    return h * window


def _bidir_long_conv(u, filt, bias):
    L = u.shape[1]
    hf, hb = filt[:, 0], filt[:, 1]
    kern = jnp.concatenate([hf[:1] + hb[:1], hf[1:], jnp.zeros((1, D_HY), jnp.float32), hb[:0:-1]], axis=0)
    uf = u.astype(jnp.float32)
    U = jnp.fft.rfft(uf, n=2 * L, axis=1)
    K = jnp.fft.rfft(kern, n=2 * L, axis=0)
    y = jnp.fft.irfft(U * K[None], n=2 * L, axis=1)[:, :L]
    return (y + uf * bias.astype(jnp.float32)).astype(u.dtype)


def _modulate(x, shift, scale):
    return x * (1.0 + scale[:, None, :]) + shift[:, None, :]


def _trunk(x, c, ada_w, ada_b, ffn1_wi, ffn1_wo, ffn2_wi, ffn2_wo, ln_g, ln_b, w_in, w_out, sink,
           grp_norm_g, hy_conv_w, hy_conv_b, hy_w1, hy_b1, hy_w2, hy_b2, hy_w3, hy_b3, hy_w4,
           hy_freq, hy_decay, hy_bias):
    B, L = x.shape[0], x.shape[1]
    for l in range(DEPTH):
        mod = (jax.nn.silu(c) @ ada_w[l] + ada_b[l]).reshape(B, N_MOD, D_MODEL)
        h = _modulate(x, mod[:, 0], mod[:, 1])
        f = _swiglu(h, ffn1_wi[l], ffn1_wo[l])
        x = _layer_norm(ALPHA * x + 0.5 * (1.0 + mod[:, 2][:, None, :]) * f, ln_g[l, 0], ln_b[l, 0])
        h = _modulate(x, mod[:, 3], mod[:, 4])
        z = h @ w_in[l]
        q, k, v, hz = jnp.split(z, [D_ATTN, D_ATTN + KV_WIDTH, D_ATTN + 2 * KV_WIDTH], axis=-1)
        q = _partial_rope(q.reshape(B, L, N_Q_HEADS, HEAD_DIM), L)
        k = _partial_rope(k.reshape(B, L, N_KV_HEADS, HEAD_DIM), L)
        v = v.reshape(B, L, N_KV_HEADS, HEAD_DIM)
        o_attn = _window_attention(q, k, v, sink[l])
        hz = _short_conv(hz, hy_conv_w[l], hy_conv_b[l])
        x0, x1, hv = jnp.split(hz, 3, axis=-1)
        filt = _hyena_filters(L, hy_w1[l], hy_b1[l], hy_w2[l], hy_b2[l], hy_w3[l], hy_b3[l], hy_w4[l],
                              hy_freq[l], hy_decay[l])
        o_hy = _bidir_long_conv(hv * x1, filt, hy_bias[l]) * x0
        o = jnp.concatenate([_rms_norm(o_attn, grp_norm_g[l, :D_ATTN]),
                             _rms_norm(o_hy, grp_norm_g[l, D_ATTN:])], axis=-1) @ w_out[l]
        x = _layer_norm(ALPHA * x + (1.0 + mod[:, 5][:, None, :]) * o, ln_g[l, 1], ln_b[l, 1])
        h = _modulate(x, mod[:, 6], mod[:, 7])
        f = _swiglu(h, ffn2_wi[l], ffn2_wo[l])
        x = _layer_norm(ALPHA * x + 0.5 * (1.0 + mod[:, 8][:, None, :]) * f, ln_g[l, 2], ln_b[l, 2])
    return x


def setup_inputs(seed: int = 0) -> dict:
    key = jax.random.key(seed)
    ks = jax.random.split(key, 32)
    f32 = jnp.float32
    n = lambda i, shape, s: jax.random.normal(ks[i], shape, f32) * s
    base_decay = jnp.abs(jnp.linspace(math.log(DECAY_TARGET) / FAST_DECAY_PCT,
                                      math.log(DECAY_TARGET) / SLOW_DECAY_PCT, D_HY, dtype=f32))
    return {
        "x_prompt": n(0, (BATCH, SEQ, D_MODEL), 1.0),
        "x_sample": n(1, (DEC_BATCH, DEC_SEQ, D_MODEL), 1.0),
        "c_prompt": n(2, (BATCH, D_MODEL), 1.0),
        "c_sample": n(3, (DEC_BATCH, D_MODEL), 1.0),
        "ada_w": n(4, (DEPTH, D_MODEL, N_MOD * D_MODEL), 0.5 * D_MODEL ** -0.5),
        "ada_b": n(5, (DEPTH, N_MOD * D_MODEL), 0.01),
        "ffn1_wi": n(6, (DEPTH, D_MODEL, 2 * D_FF), D_MODEL ** -0.5),
        "ffn1_wo": n(7, (DEPTH, D_FF, D_MODEL), BETA * D_FF ** -0.5),
        "ffn2_wi": n(8, (DEPTH, D_MODEL, 2 * D_FF), D_MODEL ** -0.5),
        "ffn2_wo": n(9, (DEPTH, D_FF, D_MODEL), BETA * D_FF ** -0.5),
        "ln_g": 1.0 + n(10, (DEPTH, 3, D_MODEL), 0.02),
        "ln_b": n(11, (DEPTH, 3, D_MODEL), 0.02),
        "w_in": n(12, (DEPTH, D_MODEL, N_IN), D_MODEL ** -0.5),
        "w_out": n(13, (DEPTH, D_MIX, D_MODEL), BETA * D_MIX ** -0.5),
        "sink": n(14, (DEPTH, N_Q_HEADS), 1.0),
        "grp_norm_g": 1.0 + n(15, (DEPTH, D_MIX), 0.02),
        "hy_conv_w": n(16, (DEPTH, SHORT_K, (HY_ORDER + 1) * D_HY), SHORT_K ** -0.5),
        "hy_conv_b": n(17, (DEPTH, (HY_ORDER + 1) * D_HY), 0.02),
        "hy_w1": n(18, (DEPTH, FILTER_EMB, FILTER_HID), FILTER_EMB ** -0.5),
        "hy_b1": n(19, (DEPTH, FILTER_HID), 0.02),
        "hy_w2": n(20, (DEPTH, FILTER_HID, FILTER_HID), FILTER_HID ** -0.5),
        "hy_b2": n(21, (DEPTH, FILTER_HID), 0.02),
        "hy_w3": n(22, (DEPTH, FILTER_HID, FILTER_HID), FILTER_HID ** -0.5),
        "hy_b3": n(23, (DEPTH, FILTER_HID), 0.02),
        "hy_w4": n(24, (DEPTH, FILTER_HID, 2 * D_HY), FILTER_HID ** -0.5),
        "hy_freq": 1.0 + n(25, (DEPTH, FILTER_HID), 0.05),
        "hy_decay": base_decay[None, None, :] * (1.0 + n(26, (DEPTH, 2, D_HY), 0.05)),
        "hy_bias": n(27, (DEPTH, D_HY), 0.5),
    }


def reference(x_prompt, x_sample, c_prompt, c_sample, ada_w, ada_b, ffn1_wi, ffn1_wo, ffn2_wi, ffn2_wo,
              ln_g, ln_b, w_in, w_out, sink, grp_norm_g, hy_conv_w, hy_conv_b, hy_w1, hy_b1, hy_w2, hy_b2,
              hy_w3, hy_b3, hy_w4, hy_freq, hy_decay, hy_bias):
    y_prompt = _trunk(x_prompt, c_prompt, ada_w, ada_b, ffn1_wi, ffn1_wo, ffn2_wi, ffn2_wo, ln_g, ln_b,
                      w_in, w_out, sink, grp_norm_g, hy_conv_w, hy_conv_b, hy_w1, hy_b1, hy_w2, hy_b2,
                      hy_w3, hy_b3, hy_w4, hy_freq, hy_decay, hy_bias)
    y_sample = _trunk(x_sample, c_sample, ada_w, ada_b, ffn1_wi, ffn1_wo, ffn2_wi, ffn2_wo, ln_g, ln_b,
                      w_in, w_out, sink, grp_norm_g, hy_conv_w, hy_conv_b, hy_w1, hy_b1, hy_w2, hy_b2,
                      hy_w3, hy_b3, hy_w4, hy_freq, hy_decay, hy_bias)
    return (y_prompt, y_sample)
```

```python
import functools
import math

import jax
import jax.numpy as jnp
from jax import lax
from jax.experimental import pallas as pl
from jax.experimental.pallas import tpu as pltpu

F32 = jnp.float32
BF16 = jnp.bfloat16

D_MODEL = 1024
D_ATTN = 512
D_HY = 512
HEAD_DIM = 64
N_Q_HEADS = D_ATTN // HEAD_DIM
N_KV_HEADS = 2
KV_WIDTH = N_KV_HEADS * HEAD_DIM
ROT_DIM = HEAD_DIM // 4
ROPE_THETA = 500000.0
WINDOW = 128
FILTER_EMB = 33
FILTER_HID = 64
D_FF = 2816
N_IN = D_ATTN + 2 * KV_WIDTH + 3 * D_HY
N_MOD = 9
DEPTH = 4
ALPHA = float((2 * DEPTH) ** 0.25)
LN_EPS = 1e-5
RMS_EPS = 1e-6
NEG_BIG = -1e30

LANES = 128
VMEM_BYTES = 64 * 1024 * 1024
MIB = 1024 * 1024

ROW_TILE = 512
FF_CHUNK = 256
Q_TILE = 512
Q_BLOCK = 128
BAND = 3 * Q_BLOCK
FREQ_TILE = 256
TIME_TILE = 256
PREP_TILE = 512
MOD_N_TILE = 1536
HALO = 8


def _params(semantics, vmem_mib):
    return pltpu.CompilerParams(dimension_semantics=semantics,
                                vmem_limit_bytes=min(vmem_mib * MIB, VMEM_BYTES - 4 * MIB))


def _resident(shape):
    nd = len(shape)
    return pl.BlockSpec(shape, lambda *_: (0,) * nd, pipeline_mode=pl.Buffered(1))


def _dot(a, b):
    return jnp.dot(a, b, preferred_element_type=F32)


def _split(a):
    hi = a.astype(BF16)
    lo = (a - hi.astype(F32)).astype(BF16)
    return hi, lo


def _dot3(a, b):
    a_hi, a_lo = _split(a)
    b_hi, b_lo = _split(b)
    return _dot(a_hi, b_hi) + _dot(a_hi, b_lo) + _dot(a_lo, b_hi)


def _layer_norm(y, g, b):
    mu = jnp.mean(y, axis=-1, keepdims=True)
    d = y - mu
    var = jnp.mean(d * d, axis=-1, keepdims=True)
    return d * lax.rsqrt(var + LN_EPS) * g + b


def _rms_norm(y, g):
    ms = jnp.mean(y * y, axis=-1, keepdims=True)
    return y * lax.rsqrt(ms + RMS_EPS) * g


def _mod_kernel(c_ref, w_ref, b_ref, o_ref):
    c = c_ref[...]
    a = c * jax.nn.sigmoid(c)
    o_ref[0] = _dot3(a, w_ref[0]) + b_ref[0]


def _modulation(c, ada_w, ada_b):
    depth = ada_w.shape[0]
    b = c.shape[0]
    n = N_MOD * D_MODEL
    out = pl.pallas_call(
        _mod_kernel,
        out_shape=jax.ShapeDtypeStruct((depth, b, n), F32),
        grid=(depth, n // MOD_N_TILE),
        in_specs=[
            pl.BlockSpec((b, D_MODEL), lambda l, j: (0, 0)),
            pl.BlockSpec((1, D_MODEL, MOD_N_TILE), lambda l, j: (l, 0, j)),
            pl.BlockSpec((1, 1, MOD_N_TILE), lambda l, j: (l, 0, j)),
        ],
        out_specs=pl.BlockSpec((1, b, MOD_N_TILE), lambda l, j: (l, 0, j)),
        compiler_params=_params(("arbitrary", "arbitrary"), 40),
        name="adaln_mod",
    )(c, ada_w, ada_b.reshape(depth, 1, n))
    return out.reshape(depth, b, N_MOD, D_MODEL)


def _ffn_kernel(x_ref, mod_ref, wi_ref, wo_ref, g_ref, b_ref, o_ref, h_scr, acc_scr, *, mod_row):
    x = x_ref[0]
    m = mod_ref[0]
    shift, scale, gate = m[mod_row:mod_row + 1], m[mod_row + 1:mod_row + 2], m[mod_row + 2:mod_row + 3]
    h_scr[...] = (x * (1.0 + scale) + shift).astype(BF16)
    acc_scr[...] = jnp.zeros_like(acc_scr)

    def chunk(c, carry):
        gu = _dot(h_scr[...], wi_ref[c])
        g, u = gu[:, :FF_CHUNK], gu[:, FF_CHUNK:]
        a = (g * jax.nn.sigmoid(g) * u).astype(BF16)
        acc_scr[...] += _dot(a, wo_ref[c])
        return carry

    lax.fori_loop(0, D_FF // FF_CHUNK, chunk, 0)
    y = ALPHA * x + 0.5 * (1.0 + gate) * acc_scr[...]
    o_ref[0] = _layer_norm(y, g_ref[...], b_ref[...])


def _ffn(x, mod, wi_c, wo_c, ln_g, ln_b, mod_row):
    b, l, _ = x.shape
    tm = min(ROW_TILE, l)
    nck = D_FF // FF_CHUNK
    return pl.pallas_call(
        functools.partial(_ffn_kernel, mod_row=mod_row),
        out_shape=jax.ShapeDtypeStruct(x.shape, F32),
        grid=(b, l // tm),
        in_specs=[
            pl.BlockSpec((1, tm, D_MODEL), lambda i, j: (i, j, 0)),
            pl.BlockSpec((1, N_MOD, D_MODEL), lambda i, j: (i, 0, 0)),
            _resident((nck, D_MODEL, 2 * FF_CHUNK)),
            _resident((nck, FF_CHUNK, D_MODEL)),
            _resident((1, D_MODEL)),
            _resident((1, D_MODEL)),
        ],
        out_specs=pl.BlockSpec((1, tm, D_MODEL), lambda i, j: (i, j, 0)),
        scratch_shapes=[pltpu.VMEM((tm, D_MODEL), BF16), pltpu.VMEM((tm, D_MODEL), F32)],
        compiler_params=_params(("arbitrary", "arbitrary"), 48),
        name="ffn",
    )(x, mod, wi_c, wo_c, ln_g, ln_b)


def _rope_slab(t, cos, sin_up, sin_dn):
    return (t * cos + pltpu.roll(t, ROT_DIM // 2, 1) * sin_up
            + pltpu.roll(t, LANES - ROT_DIM // 2, 1) * sin_dn)


def _in_proj_kernel(x_ref, mod_ref, w_ref, cos_ref, sup_ref, sdn_ref,
                    q_ref, kd_ref, vd_ref, hz_ref):
    x = x_ref[0]
    m = mod_ref[0]
    h = (x * (1.0 + m[4:5]) + m[3:4]).astype(BF16)
    z = _dot(h, w_ref[...])
    cos, sup, sdn = cos_ref[...], sup_ref[...], sdn_ref[...]
    for s in range(D_ATTN // LANES):
        q = _rope_slab(z[:, s * LANES:(s + 1) * LANES], cos, sup, sdn)
        q_ref[0, :, s * LANES:(s + 1) * LANES] = (q * (HEAD_DIM ** -0.5)).astype(BF16)
    k = _rope_slab(z[:, D_ATTN:D_ATTN + KV_WIDTH], cos, sup, sdn)
    v = z[:, D_ATTN + KV_WIDTH:D_ATTN + 2 * KV_WIDTH]
    lane = lax.broadcasted_iota(jnp.int32, k.shape, 1)
    first = lane < HEAD_DIM
    for t, ref in ((k, kd_ref), (v, vd_ref)):
        swapped = pltpu.roll(t, HEAD_DIM, 1)
        ref[0, :, :LANES] = jnp.where(first, t, swapped).astype(BF16)
        ref[0, :, LANES:] = jnp.where(first, swapped, t).astype(BF16)
    hz_ref[0] = z[:, D_ATTN + 2 * KV_WIDTH:]


def _in_proj(x, mod, w_in, rope):
    b, l, _ = x.shape
    tm = min(ROW_TILE, l)
    row = lambda i, j: (i, j, 0)
    tab = pl.BlockSpec((tm, LANES), lambda i, j: (j, 0))
    return pl.pallas_call(
        _in_proj_kernel,
        out_shape=(
            jax.ShapeDtypeStruct((b, l, D_ATTN), BF16),
            jax.ShapeDtypeStruct((b, l, 2 * KV_WIDTH), BF16),
            jax.ShapeDtypeStruct((b, l, 2 * KV_WIDTH), BF16),
            jax.ShapeDtypeStruct((b, l, 3 * D_HY), F32),
        ),
        grid=(b, l // tm),
        in_specs=[
            pl.BlockSpec((1, tm, D_MODEL), row),
            pl.BlockSpec((1, N_MOD, D_MODEL), lambda i, j: (i, 0, 0)),
            _resident((D_MODEL, N_IN)),
            tab, tab, tab,
        ],
        out_specs=(
            pl.BlockSpec((1, tm, D_ATTN), row),
            pl.BlockSpec((1, tm, 2 * KV_WIDTH), row),
            pl.BlockSpec((1, tm, 2 * KV_WIDTH), row),
            pl.BlockSpec((1, tm, 3 * D_HY), row),
        ),
        compiler_params=_params(("arbitrary", "arbitrary"), 48),
        name="in_proj",
    )(x, mod, w_in, *rope)


def _rope_tables(l):
    half = ROT_DIM // 2
    inv = ROPE_THETA ** (-jnp.arange(0, ROT_DIM, 2, dtype=F32) / ROT_DIM)
    ang = jnp.arange(l, dtype=F32)[:, None] * inv[None]
    cos, sin = jnp.cos(ang), jnp.sin(ang)
    rest = HEAD_DIM - ROT_DIM
    cos_h = jnp.concatenate([cos, cos, jnp.ones((l, rest), F32)], axis=1)
    up_h = jnp.concatenate([jnp.zeros((l, half), F32), sin, jnp.zeros((l, rest), F32)], axis=1)
    dn_h = jnp.concatenate([-sin, jnp.zeros((l, half + rest), F32)], axis=1)
    reps = LANES // HEAD_DIM
    return tuple(jnp.tile(t, (1, reps)) for t in (cos_h, up_h, dn_h))


def _attn_kernel(sink_ref, q_ref, kd_ref, vd_ref, o_ref, *, seq):
    j = pl.program_id(1)
    lane = lax.broadcasted_iota(jnp.int32, (BAND, LANES), 1)
    first = lane < HEAD_DIM
    out_first = lax.broadcasted_iota(jnp.int32, (Q_BLOCK, LANES), 1) < HEAD_DIM
    for blk in range(q_ref.shape[1] // Q_BLOCK):
        q0 = j * q_ref.shape[1] + blk * Q_BLOCK
        start = pl.multiple_of(jnp.clip(q0 - Q_BLOCK, 0, seq - BAND), Q_BLOCK)
        qpos = q0 + lax.broadcasted_iota(jnp.int32, (Q_BLOCK, BAND), 0)
        kpos = start + lax.broadcasted_iota(jnp.int32, (Q_BLOCK, BAND), 1)
        bias = jnp.where(jnp.abs(qpos - kpos) <= WINDOW, 0.0, NEG_BIG).astype(F32)
        rows = slice(blk * Q_BLOCK, (blk + 1) * Q_BLOCK)
        for kh in range(N_KV_HEADS):
            cols = slice(kh * LANES, (kh + 1) * LANES)
            kslab = kd_ref[0, pl.ds(start, BAND), cols]
            vslab = vd_ref[0, pl.ds(start, BAND), cols]
            zero = jnp.zeros_like(kslab)
            kb = jnp.concatenate([jnp.where(first, kslab, zero), jnp.where(first, zero, kslab)], axis=0)
            vb = jnp.concatenate([jnp.where(first, vslab, zero), jnp.where(first, zero, vslab)], axis=0)
            for half in range(2):
                slab = 2 * kh + half
                q = q_ref[0, rows, slab * LANES:(slab + 1) * LANES]
                s = lax.dot_general(q, kb, (((1,), (1,)), ((), ())), preferred_element_type=F32)
                probs, inv = [], []
                for e in range(2):
                    sink = sink_ref[2 * slab + e]
                    se = s[:, e * BAND:(e + 1) * BAND] + bias
                    mx = jnp.maximum(jnp.max(se, axis=1, keepdims=True), sink)
                    p = jnp.exp(se - mx)
                    den = jnp.sum(p, axis=1, keepdims=True) + jnp.exp(sink - mx)
                    probs.append(p.astype(BF16))
                    inv.append(1.0 / den)
                o = _dot(jnp.concatenate(probs, axis=1), vb)
                o = o * jnp.where(out_first, inv[0], inv[1])
                o_ref[0, rows, slab * LANES:(slab + 1) * LANES] = o.astype(o_ref.dtype)


def _attention(q, kd, vd, sink):
    b, l, _ = q.shape
    tq = min(Q_TILE, l)
    return pl.pallas_call(
        functools.partial(_attn_kernel, seq=l),
        out_shape=jax.ShapeDtypeStruct((b, l, D_ATTN), F32),
        grid=(b, l // tq),
        in_specs=[
            pl.BlockSpec(memory_space=pltpu.SMEM),
            pl.BlockSpec((1, tq, D_ATTN), lambda i, j: (i, j, 0)),
            pl.BlockSpec((1, l, 2 * KV_WIDTH), lambda i, j: (i, 0, 0)),
            pl.BlockSpec((1, l, 2 * KV_WIDTH), lambda i, j: (i, 0, 0)),
        ],
        out_specs=pl.BlockSpec((1, tq, D_ATTN), lambda i, j: (i, j, 0)),
        compiler_params=_params(("arbitrary", "arbitrary"), 32),
        name="window_attn",
    )(sink, q, kd, vd)


def _filter_kernel(z_ref, w1_ref, b1_ref, w2_ref, b2_ref, w3_ref, b3_ref, w4_ref, fr_ref, dec_ref,
                   e_ref, d_ref):
    z = z_ref[...]
    fr = fr_ref[...]
    h = jnp.sin(fr * (_dot3(z, w1_ref[...]) + b1_ref[...]))
    h = jnp.sin(fr * (_dot3(h, w2_ref[...]) + b2_ref[...]))
    h = jnp.sin(fr * (_dot3(h, w3_ref[...]) + b3_ref[...]))
    filt = _dot3(h, w4_ref[...]) * jnp.exp(-z[:, 0:1] * jnp.abs(dec_ref[...]))
    fwd, bwd = filt[:, :D_HY], filt[:, D_HY:]
    e_ref[...] = (fwd + bwd).astype(BF16)
    d_ref[...] = (fwd - bwd).astype(BF16)


def _filter_features(l):
    t = jnp.linspace(0.0, 1.0, l, dtype=F32)[:, None]
    n_bands = (FILTER_EMB - 1) // 2
    w = 2.0 * math.pi * jnp.arange(l, dtype=F32)[:, None] / l
    fb = jnp.linspace(1e-4, n_bands - 1, n_bands, dtype=F32)[None]
    z = jnp.concatenate([t, jnp.cos(fb * w), -jnp.sin(fb * w)], axis=-1)
    return jnp.pad(z, ((0, 0), (0, FILTER_HID - FILTER_EMB)))


def _filters(z, w1, b1, w2, b2, w3, b3, w4, freq, decay):
    l = z.shape[0]
    tl = min(ROW_TILE, l)
    w1p = jnp.pad(w1, ((0, FILTER_HID - FILTER_EMB), (0, 0)))
    vec = lambda a: a.reshape(1, -1)
    out = jax.ShapeDtypeStruct((l, D_HY), BF16)
    tile = pl.BlockSpec((tl, D_HY), lambda i: (i, 0))
    return pl.pallas_call(
        _filter_kernel,
        out_shape=(out, out),
        grid=(l // tl,),
        in_specs=[
            pl.BlockSpec((tl, FILTER_HID), lambda i: (i, 0)),
            _resident((FILTER_HID, FILTER_HID)), _resident((1, FILTER_HID)),
            _resident((FILTER_HID, FILTER_HID)), _resident((1, FILTER_HID)),
            _resident((FILTER_HID, FILTER_HID)), _resident((1, FILTER_HID)),
            _resident((FILTER_HID, 2 * D_HY)), _resident((1, FILTER_HID)), _resident((1, 2 * D_HY)),
        ],
        out_specs=(tile, tile),
        compiler_params=_params(("arbitrary",), 32),
        name="hyena_filter",
    )(z, w1p, vec(b1), w2, vec(b2), w3, vec(b3), w4, vec(freq), vec(decay))


def _spectrum_kernel(m_ref, e_ref, d_ref, ka_ref, kb_ref, ka2_ref, *, seq):
    i = pl.program_id(0)
    tf = ka_ref.shape[0]
    mat = m_ref[0]
    re = _dot(mat[:tf], e_ref[...])
    im = _dot(mat[tf:], d_ref[...])
    nyq = _dot(mat[tf:tf + HALO], e_ref[...])[0:1]
    is_dc = (lax.broadcasted_iota(jnp.int32, (tf, 1), 0) == 0) & (i == 0)
    scale = jnp.where(is_dc, 0.5 / seq, 1.0 / seq)
    ka = re * scale
    ka_ref[...] = ka
    kb_ref[...] = jnp.where(is_dc, 0.0, im * scale)
    ka2_ref[...] = jnp.where(is_dc, nyq * (0.5 / seq), ka)


def _spectrum(mfwd, e, d):
    nf, tf2, l = mfwd.shape
    tf = tf2 // 2
    out = jax.ShapeDtypeStruct((l, D_HY), F32)
    tile = pl.BlockSpec((tf, D_HY), lambda i: (i, 0))
    return pl.pallas_call(
        functools.partial(_spectrum_kernel, seq=l),
        out_shape=(out, out, out),
        grid=(nf,),
        in_specs=[pl.BlockSpec((1, tf2, l), lambda i: (i, 0, 0)), _resident((l, D_HY)), _resident((l, D_HY))],
        out_specs=(tile, tile, tile),
        compiler_params=_params(("arbitrary",), 40),
        name="hyena_spectrum",
    )(mfwd, e, d)


def _dft_matrices(l, tf):
    nf = l // tf
    j = jnp.arange(l, dtype=jnp.int32)

    def trig(f):
        idx = (f[:, None] * j[None, :]) % (2 * l)
        ang = idx.astype(F32) * (math.pi / l)
        return jnp.cos(ang), jnp.sin(ang)

    ac, as_ = trig(jnp.arange(nf, dtype=jnp.int32) * tf)
    bc, bs = trig(jnp.arange(tf, dtype=jnp.int32))
    cosm = ac[:, None, :] * bc[None] - as_[:, None, :] * bs[None]
    sinm = as_[:, None, :] * bc[None] + ac[:, None, :] * bs[None]
    alt = jnp.where(j % 2 == 0, 1.0, -1.0).astype(F32)
    sinm = sinm.at[0, 0, :].set(alt)
    mfwd = jnp.concatenate([cosm, sinm], axis=1).astype(BF16)
    return mfwd, mfwd.reshape(2 * l, l).T


def _dft_fwd_kernel(hz_ref, hzp_ref, hzn_ref, cw_ref, cb_ref, m_ref, ka_ref, kb_ref, ka2_ref,
                    y_ref, p_ref, x0_ref, p_scr, *, nprep):
    s = pl.program_id(1)
    tr = hz_ref.shape[1]
    tf = ka_ref.shape[0]

    @pl.when(s < nprep)
    def _():
        z = hz_ref[0]
        prev = jnp.where(s == 0, 0.0, hzp_ref[0][HALO - 1:HALO])
        nxt = jnp.where(s == nprep - 1, 0.0, hzn_ref[0][0:1])
        row = lax.broadcasted_iota(jnp.int32, (tr, 1), 0)
        zp = jnp.where(row == 0, prev, pltpu.roll(z, 1, 0))
        zn = jnp.where(row == tr - 1, nxt, pltpu.roll(z, tr - 1, 0))
        w = cw_ref[...]
        c = zp * w[0:1] + z * w[1:2] + zn * w[2:3] + cb_ref[...]
        p = c[:, 2 * D_HY:] * c[:, D_HY:2 * D_HY]
        p_ref[0] = p
        x0_ref[0] = c[:, :D_HY]
        p_scr[pl.ds(pl.multiple_of(s * tr, tr), tr), :] = p.astype(BF16)

    @pl.when(s >= nprep)
    def _():
        u = _dot(m_ref[0], p_scr[...])
        ure, uim = u[:tf], u[tf:]
        ka, kb, ka2 = ka_ref[...], kb_ref[...], ka2_ref[...]
        y_ref[0, 0, :tf] = (ure * ka - uim * kb).astype(BF16)
        y_ref[0, 0, tf:] = (ure * kb + uim * ka2).astype(BF16)


def _dft_fwd(hz, conv_w, conv_b, mfwd, ka, kb, ka2):
    b, l, _ = hz.shape
    nf, tf2, _ = mfwd.shape
    tf = tf2 // 2
    tr = min(PREP_TILE, l)
    nprep = l // tr
    per = tr // HALO
    last = l // HALO - 1
    prep = lambda s: jnp.minimum(s, nprep - 1)
    freq = lambda s: jnp.maximum(s - nprep, 0)
    ktile = pl.BlockSpec((tf, D_HY), lambda i, s: (freq(s), 0))
    rtile = pl.BlockSpec((1, tr, D_HY), lambda i, s: (i, prep(s), 0))
    return pl.pallas_call(
        functools.partial(_dft_fwd_kernel, nprep=nprep),
        out_shape=(
            jax.ShapeDtypeStruct((b, nf, tf2, D_HY), BF16),
            jax.ShapeDtypeStruct((b, l, D_HY), F32),
            jax.ShapeDtypeStruct((b, l, D_HY), F32),
        ),
        grid=(b, nprep + nf),
        in_specs=[
            pl.BlockSpec((1, tr, 3 * D_HY), lambda i, s: (i, prep(s), 0)),
            pl.BlockSpec((1, HALO, 3 * D_HY), lambda i, s: (i, jnp.maximum(prep(s) * per - 1, 0), 0)),
            pl.BlockSpec((1, HALO, 3 * D_HY), lambda i, s: (i, jnp.minimum((prep(s) + 1) * per, last), 0)),
            _resident((3, 3 * D_HY)),
            _resident((1, 3 * D_HY)),
            pl.BlockSpec((1, tf2, l), lambda i, s: (freq(s), 0, 0)),
            ktile, ktile, ktile,
        ],
        out_specs=(
            pl.BlockSpec((1, 1, tf2, D_HY), lambda i, s: (i, freq(s), 0, 0)),
            rtile, rtile,
        ),
        scratch_shapes=[pltpu.VMEM((l, D_HY), BF16)],
        compiler_params=_params(("arbitrary", "arbitrary"), 48),
        name="hyena_dft_fwd",
    )(hz, hz, hz, conv_w, conv_b.reshape(1, -1), mfwd, ka, kb, ka2)


def _dft_inv_kernel(g_ref, y_ref, p_ref, x0_ref, bias_ref, o_ref):
    y = _dot(g_ref[...], y_ref[0])
    o_ref[0] = (y + p_ref[0] * bias_ref[...]) * x0_ref[0]


def _dft_inv(ginv, y, p, x0, bias):
    b, l, _ = p.shape
    ti = min(TIME_TILE, l)
    tile = pl.BlockSpec((1, ti, D_HY), lambda i, j: (i, j, 0))
    return pl.pallas_call(
        _dft_inv_kernel,
        out_shape=jax.ShapeDtypeStruct((b, l, D_HY), F32),
        grid=(b, l // ti),
        in_specs=[
            pl.BlockSpec((ti, 2 * l), lambda i, j: (j, 0)),
            pl.BlockSpec((1, 2 * l, D_HY), lambda i, j: (i, 0, 0)),
            tile, tile,
            _resident((1, D_HY)),
        ],
        out_specs=tile,
        compiler_params=_params(("arbitrary", "arbitrary"), 48),
        name="hyena_dft_inv",
    )(ginv, y, p, x0, bias.reshape(1, -1))


def _out_proj_kernel(x_ref, mod_ref, oa_ref, oh_ref, w_ref, gn_ref, g_ref, b_ref, o_ref):
    x = x_ref[0]
    gate = mod_ref[0][5:6]
    gn = gn_ref[...]
    a = _rms_norm(oa_ref[0], gn[:, :D_ATTN]).astype(BF16)
    h = _rms_norm(oh_ref[0], gn[:, D_ATTN:]).astype(BF16)
    o = _dot(a, w_ref[:D_ATTN, :]) + _dot(h, w_ref[D_ATTN:, :])
    o_ref[0] = _layer_norm(ALPHA * x + (1.0 + gate) * o, g_ref[...], b_ref[...])


def _out_proj(x, mod, o_attn, o_hy, w_out, gn, ln_g, ln_b):
    b, l, _ = x.shape
    tm = min(ROW_TILE, l)
    row = lambda i, j: (i, j, 0)
    return pl.pallas_call(
        _out_proj_kernel,
        out_shape=jax.ShapeDtypeStruct(x.shape, F32),
        grid=(b, l // tm),
        in_specs=[
            pl.BlockSpec((1, tm, D_MODEL), row),
            pl.BlockSpec((1, N_MOD, D_MODEL), lambda i, j: (i, 0, 0)),
            pl.BlockSpec((1, tm, D_ATTN), row),
            pl.BlockSpec((1, tm, D_HY), row),
            _resident((D_MODEL, D_MODEL)),
            _resident((1, D_MODEL)), _resident((1, D_MODEL)), _resident((1, D_MODEL)),
        ],
        out_specs=pl.BlockSpec((1, tm, D_MODEL), row),
        compiler_params=_params(("arbitrary", "arbitrary"), 40),
        name="out_proj",
    )(x, mod, o_attn, o_hy, w_out, gn, ln_g, ln_b)


def _chunk_ffn_weights(wi, wo):
    nck = D_FF // FF_CHUNK
    g = wi[:, :D_FF].reshape(D_MODEL, nck, FF_CHUNK)
    u = wi[:, D_FF:].reshape(D_MODEL, nck, FF_CHUNK)
    wi_c = jnp.concatenate([g, u], axis=2).transpose(1, 0, 2).astype(BF16)
    return wi_c, wo.reshape(nck, FF_CHUNK, D_MODEL).astype(BF16)


def _trunk(x, mod, weights, p):
    l = x.shape[1]
    depth = mod.shape[0]
    rope = _rope_tables(l)
    feats = _filter_features(l)
    mfwd, ginv = _dft_matrices(l, min(FREQ_TILE, l))
    vec = lambda a: a.reshape(1, -1)
    for i in range(depth):
        w = weights[i]
        m = mod[i]
        x = _ffn(x, m, w["wi1"], w["wo1"], vec(p["ln_g"][i, 0]), vec(p["ln_b"][i, 0]), 0)
        q, kd, vd, hz = _in_proj(x, m, w["w_in"], rope)
        o_attn = _attention(q, kd, vd, p["sink"][i])
        e, d = _filters(feats, p["hy_w1"][i], p["hy_b1"][i], p["hy_w2"][i], p["hy_b2"][i],
                        p["hy_w3"][i], p["hy_b3"][i], p["hy_w4"][i], p["hy_freq"][i], p["hy_decay"][i])
        ka, kb, ka2 = _spectrum(mfwd, e, d)
        y, gated, x0 = _dft_fwd(hz, p["hy_conv_w"][i], p["hy_conv_b"][i], mfwd, ka, kb, ka2)
        o_hy = _dft_inv(ginv, y.reshape(y.shape[0], 2 * l, D_HY), gated, x0, p["hy_bias"][i])
        x = _out_proj(x, m, o_attn, o_hy, w["w_out"], vec(p["grp_norm_g"][i]),
                      vec(p["ln_g"][i, 1]), vec(p["ln_b"][i, 1]))
        x = _ffn(x, m, w["wi2"], w["wo2"], vec(p["ln_g"][i, 2]), vec(p["ln_b"][i, 2]), 6)
    return x


def kernel(x_prompt, x_sample, c_prompt, c_sample, ada_w, ada_b, ffn1_wi, ffn1_wo, ffn2_wi, ffn2_wo, ln_g, ln_b, w_in, w_out, sink, grp_norm_g, hy_conv_w, hy_conv_b, hy_w1, hy_b1, hy_w2, hy_b2, hy_w3, hy_b3, hy_w4, hy_freq, hy_decay, hy_bias):
    p = dict(ln_g=ln_g, ln_b=ln_b, sink=sink, grp_norm_g=grp_norm_g, hy_conv_w=hy_conv_w,
             hy_conv_b=hy_conv_b, hy_w1=hy_w1, hy_b1=hy_b1, hy_w2=hy_w2, hy_b2=hy_b2, hy_w3=hy_w3,
             hy_b3=hy_b3, hy_w4=hy_w4, hy_freq=hy_freq, hy_decay=hy_decay, hy_bias=hy_bias)
    depth = ada_w.shape[0]
    weights = []
    for i in range(depth):
        wi1, wo1 = _chunk_ffn_weights(ffn1_wi[i], ffn1_wo[i])
        wi2, wo2 = _chunk_ffn_weights(ffn2_wi[i], ffn2_wo[i])
        weights.append(dict(wi1=wi1, wo1=wo1, wi2=wi2, wo2=wo2,
                            w_in=w_in[i].astype(BF16), w_out=w_out[i].astype(BF16)))
    nb = c_prompt.shape[0]
    mod = _modulation(jnp.concatenate([c_prompt, c_sample], axis=0), ada_w, ada_b)
    y_prompt = _trunk(x_prompt, mod[:, :nb], weights, p)
    y_sample = _trunk(x_sample, mod[:, nb:], weights, p)
    return (y_prompt, y_sample)
```

```python
import functools
import math

import jax
import jax.numpy as jnp
from jax import lax
from jax.experimental import pallas as pl
from jax.experimental.pallas import tpu as pltpu

F32 = jnp.float32
BF16 = jnp.bfloat16

D_MODEL = 1024
D_ATTN = 512
D_HY = 512
HEAD_DIM = 64
N_Q_HEADS = D_ATTN // HEAD_DIM
N_KV_HEADS = 2
KV_WIDTH = N_KV_HEADS * HEAD_DIM
ROT_DIM = HEAD_DIM // 4
ROPE_THETA = 500000.0
WINDOW = 128
FILTER_EMB = 33
FILTER_HID = 64
D_FF = 2816
N_IN = D_ATTN + 2 * KV_WIDTH + 3 * D_HY
N_MOD = 9
DEPTH = 4
ALPHA = float((2 * DEPTH) ** 0.25)
LN_EPS = 1e-5
RMS_EPS = 1e-6
NEG_BIG = -1e30
LOG2E = math.log2(math.e)
Q_SCALE = HEAD_DIM ** -0.5 * LOG2E

LANES = 128
VMEM_BYTES = 64 * 1024 * 1024
MIB = 1024 * 1024

ROW_TILE = 512
FFN_TILE = 1024
FFN_SUB = 512
FF_CHUNK = 256
Q_TILE = 512
Q_BLOCK = 128
BAND = 3 * Q_BLOCK
FREQ_TILE = 256
TIME_TILE = 256
PREP_TILE = 512
MOD_N_TILE = 1536
HALO = 16


def _params(semantics, vmem_mib):
    return pltpu.CompilerParams(dimension_semantics=semantics,
                                vmem_limit_bytes=min(vmem_mib * MIB, VMEM_BYTES - 4 * MIB))


def _resident(shape):
    nd = len(shape)
    return pl.BlockSpec(shape, lambda *_: (0,) * nd, pipeline_mode=pl.Buffered(1))


def _dot(a, b):
    return jnp.dot(a, b, preferred_element_type=F32)


def _split(a):
    hi = a.astype(BF16)
    lo = (a - hi.astype(F32)).astype(BF16)
    return hi, lo


def _dot3(a, b):
    a_hi, a_lo = _split(a)
    b_hi, b_lo = _split(b)
    return _dot(a_hi, b_hi) + _dot(a_hi, b_lo) + _dot(a_lo, b_hi)


def _layer_norm(y, g, b):
    mu = jnp.mean(y, axis=-1, keepdims=True)
    d = y - mu
    var = jnp.mean(d * d, axis=-1, keepdims=True)
    return d * lax.rsqrt(var + LN_EPS) * g + b


def _rms_norm(y, g):
    ms = jnp.mean(y * y, axis=-1, keepdims=True)
    return y * lax.rsqrt(ms + RMS_EPS) * g


def _mod_kernel(c_ref, w_ref, b_ref, o_ref):
    c = c_ref[...]
    a = c * jax.nn.sigmoid(c)
    o_ref[0] = _dot3(a, w_ref[0]) + b_ref[0]


def _modulation(c, ada_w, ada_b):
    depth = ada_w.shape[0]
    b = c.shape[0]
    n = N_MOD * D_MODEL
    out = pl.pallas_call(
        _mod_kernel,
        out_shape=jax.ShapeDtypeStruct((depth, b, n), F32),
        grid=(depth, n // MOD_N_TILE),
        in_specs=[
            pl.BlockSpec((b, D_MODEL), lambda l, j: (0, 0)),
            pl.BlockSpec((1, D_MODEL, MOD_N_TILE), lambda l, j: (l, 0, j)),
            pl.BlockSpec((1, 1, MOD_N_TILE), lambda l, j: (l, 0, j)),
        ],
        out_specs=pl.BlockSpec((1, b, MOD_N_TILE), lambda l, j: (l, 0, j)),
        compiler_params=_params(("arbitrary", "arbitrary"), 40),
        name="adaln_mod",
    )(c, ada_w, ada_b.reshape(depth, 1, n))
    return out.reshape(depth, b, N_MOD, D_MODEL)


def _ffn_kernel(x_ref, mod_ref, wi_ref, wo_ref, g_ref, b_ref, o_ref, h_scr, a_scr, *, mod_row):
    m = mod_ref[0]
    shift, scale, gate = m[mod_row:mod_row + 1], m[mod_row + 1:mod_row + 2], m[mod_row + 2:mod_row + 3]
    half_gate = 0.5 * (1.0 + gate)
    for s in range(h_scr.shape[0]):
        rows = slice(s * FFN_SUB, (s + 1) * FFN_SUB)
        h_scr[s] = (x_ref[0, rows, :] * (1.0 + scale) + shift).astype(BF16)
        for c in range(D_FF // FF_CHUNK):
            gu = _dot(h_scr[s], wi_ref[c])
            g, u = gu[:, :FF_CHUNK], gu[:, FF_CHUNK:]
            a_scr[s, :, c * FF_CHUNK:(c + 1) * FF_CHUNK] = (g * jax.nn.sigmoid(g) * u).astype(BF16)
        y = ALPHA * x_ref[0, rows, :] + half_gate * _dot(a_scr[s], wo_ref[...])
        o_ref[0, rows, :] = _layer_norm(y, g_ref[...], b_ref[...])


def _ffn(x, mod, wi_c, wo, ln_g, ln_b, mod_row):
    b, l, _ = x.shape
    tm = min(FFN_TILE, l)
    nsub = tm // FFN_SUB
    nck = D_FF // FF_CHUNK
    return pl.pallas_call(
        functools.partial(_ffn_kernel, mod_row=mod_row),
        out_shape=jax.ShapeDtypeStruct(x.shape, F32),
        grid=(b, l // tm),
        in_specs=[
            pl.BlockSpec((1, tm, D_MODEL), lambda i, j: (i, j, 0)),
            pl.BlockSpec((1, N_MOD, D_MODEL), lambda i, j: (i, 0, 0)),
            _resident((nck, D_MODEL, 2 * FF_CHUNK)),
            _resident((D_FF, D_MODEL)),
            _resident((1, D_MODEL)),
            _resident((1, D_MODEL)),
        ],
        out_specs=pl.BlockSpec((1, tm, D_MODEL), lambda i, j: (i, j, 0)),
        scratch_shapes=[pltpu.VMEM((nsub, FFN_SUB, D_MODEL), BF16), pltpu.VMEM((nsub, FFN_SUB, D_FF), BF16)],
        compiler_params=_params(("arbitrary", "arbitrary"), 56),
        name="ffn",
    )(x, mod, wi_c, wo, ln_g, ln_b)


def _rope_slab(t, cos, sin_up, sin_dn):
    return (t * cos + pltpu.roll(t, ROT_DIM // 2, 1) * sin_up
            + pltpu.roll(t, LANES - ROT_DIM // 2, 1) * sin_dn)


def _in_proj_kernel(x_ref, mod_ref, w_ref, cos_ref, sup_ref, sdn_ref,
                    q_ref, kd_ref, vd_ref, hz_ref):
    x = x_ref[0]
    m = mod_ref[0]
    h = (x * (1.0 + m[4:5]) + m[3:4]).astype(BF16)
    z = _dot(h, w_ref[...])
    cos, sup, sdn = cos_ref[...], sup_ref[...], sdn_ref[...]
    for s in range(D_ATTN // LANES):
        q = _rope_slab(z[:, s * LANES:(s + 1) * LANES], cos, sup, sdn)
        q_ref[0, :, s * LANES:(s + 1) * LANES] = (q * Q_SCALE).astype(BF16)
    k = _rope_slab(z[:, D_ATTN:D_ATTN + KV_WIDTH], cos, sup, sdn)
    v = z[:, D_ATTN + KV_WIDTH:D_ATTN + 2 * KV_WIDTH]
    lane = lax.broadcasted_iota(jnp.int32, k.shape, 1)
    first = lane < HEAD_DIM
    for t, ref in ((k, kd_ref), (v, vd_ref)):
        swapped = pltpu.roll(t, HEAD_DIM, 1)
        ref[0, :, :LANES] = jnp.where(first, t, swapped).astype(BF16)
        ref[0, :, LANES:] = jnp.where(first, swapped, t).astype(BF16)
    hz_ref[0] = z[:, D_ATTN + 2 * KV_WIDTH:].astype(BF16)


def _in_proj(x, mod, w_in, rope):
    b, l, _ = x.shape
    tm = min(ROW_TILE, l)
    row = lambda i, j: (i, j, 0)
    tab = pl.BlockSpec((tm, LANES), lambda i, j: (j, 0))
    return pl.pallas_call(
        _in_proj_kernel,
        out_shape=(
            jax.ShapeDtypeStruct((b, l, D_ATTN), BF16),
            jax.ShapeDtypeStruct((b, l, 2 * KV_WIDTH), BF16),
            jax.ShapeDtypeStruct((b, l, 2 * KV_WIDTH), BF16),
            jax.ShapeDtypeStruct((b, l, 3 * D_HY), BF16),
        ),
        grid=(b, l // tm),
        in_specs=[
            pl.BlockSpec((1, tm, D_MODEL), row),
            pl.BlockSpec((1, N_MOD, D_MODEL), lambda i, j: (i, 0, 0)),
            _resident((D_MODEL, N_IN)),
            tab, tab, tab,
        ],
        out_specs=(
            pl.BlockSpec((1, tm, D_ATTN), row),
            pl.BlockSpec((1, tm, 2 * KV_WIDTH), row),
            pl.BlockSpec((1, tm, 2 * KV_WIDTH), row),
            pl.BlockSpec((1, tm, 3 * D_HY), row),
        ),
        compiler_params=_params(("arbitrary", "arbitrary"), 48),
        name="in_proj",
    )(x, mod, w_in, *rope)


def _rope_tables(l):
    half = ROT_DIM // 2
    inv = ROPE_THETA ** (-jnp.arange(0, ROT_DIM, 2, dtype=F32) / ROT_DIM)
    ang = jnp.arange(l, dtype=F32)[:, None] * inv[None]
    cos, sin = jnp.cos(ang), jnp.sin(ang)
    rest = HEAD_DIM - ROT_DIM
    cos_h = jnp.concatenate([cos, cos, jnp.ones((l, rest), F32)], axis=1)
    up_h = jnp.concatenate([jnp.zeros((l, half), F32), sin, jnp.zeros((l, rest), F32)], axis=1)
    dn_h = jnp.concatenate([-sin, jnp.zeros((l, half + rest), F32)], axis=1)
    reps = LANES // HEAD_DIM
    return tuple(jnp.tile(t, (1, reps)) for t in (cos_h, up_h, dn_h))


def _attn_kernel(sink_ref, q_ref, kd_ref, vd_ref, o_ref, *, seq):
    j = pl.program_id(1)
    first = lax.broadcasted_iota(jnp.int32, (BAND, LANES), 1) < HEAD_DIM
    out_first = lax.broadcasted_iota(jnp.int32, (Q_BLOCK, LANES), 1) < HEAD_DIM
    zero = jnp.zeros((BAND, LANES), BF16)
    ones_ext = jnp.concatenate([jnp.where(first, 1.0, 0.0), jnp.where(first, 0.0, 1.0)], axis=0).astype(BF16)
    for blk in range(q_ref.shape[1] // Q_BLOCK):
        q0 = j * q_ref.shape[1] + blk * Q_BLOCK
        start = pl.multiple_of(jnp.clip(q0 - Q_BLOCK, 0, seq - BAND), Q_BLOCK)
        qpos = q0 + lax.broadcasted_iota(jnp.int32, (Q_BLOCK, BAND), 0)
        kpos = start + lax.broadcasted_iota(jnp.int32, (Q_BLOCK, BAND), 1)
        bias = jnp.where(jnp.abs(qpos - kpos) <= WINDOW, 0.0, NEG_BIG).astype(F32)
        rows = slice(blk * Q_BLOCK, (blk + 1) * Q_BLOCK)
        for kh in range(N_KV_HEADS):
            cols = slice(kh * LANES, (kh + 1) * LANES)
            kslab = kd_ref[0, pl.ds(start, BAND), cols]
            vslab = vd_ref[0, pl.ds(start, BAND), cols]
            kb = jnp.concatenate([jnp.where(first, kslab, zero), jnp.where(first, zero, kslab)], axis=0)
            vb = jnp.concatenate([jnp.where(first, vslab, zero), jnp.where(first, zero, vslab)], axis=0)
            vb_ext = jnp.concatenate([vb, ones_ext], axis=1)
            q = jnp.concatenate([q_ref[0, rows, (2 * kh + r) * LANES:(2 * kh + r + 1) * LANES] for r in range(2)],
                                axis=0)
            s = lax.dot_general(q, kb, (((1,), (1,)), ((), ())), preferred_element_type=F32)
            probs, sink_terms = [], []
            for r in range(2):
                pr, sk = [], []
                for e in range(2):
                    sink = sink_ref[4 * kh + 2 * r + e] * LOG2E
                    se = s[r * Q_BLOCK:(r + 1) * Q_BLOCK, e * BAND:(e + 1) * BAND] + bias
                    mx = jnp.maximum(jnp.max(se, axis=1, keepdims=True), sink)
                    pr.append(jnp.exp2(se - mx).astype(BF16))
                    sk.append(jnp.exp2(sink - mx))
                probs.append(jnp.concatenate(pr, axis=1))
                sink_terms.append(jnp.where(out_first, sk[0], sk[1]))
            o = _dot(jnp.concatenate(probs, axis=0), vb_ext)
            for r in range(2):
                part = o[r * Q_BLOCK:(r + 1) * Q_BLOCK]
                den = part[:, LANES:] + sink_terms[r]
                o_ref[0, rows, (2 * kh + r) * LANES:(2 * kh + r + 1) * LANES] = (
                    part[:, :LANES] / den).astype(o_ref.dtype)


def _attention(q, kd, vd, sink):
    b, l, _ = q.shape
    tq = min(Q_TILE, l)
    return pl.pallas_call(
        functools.partial(_attn_kernel, seq=l),
        out_shape=jax.ShapeDtypeStruct((b, l, D_ATTN), BF16),
        grid=(b, l // tq),
        in_specs=[
            pl.BlockSpec(memory_space=pltpu.SMEM),
            pl.BlockSpec((1, tq, D_ATTN), lambda i, j: (i, j, 0)),
            pl.BlockSpec((1, l, 2 * KV_WIDTH), lambda i, j: (i, 0, 0)),
            pl.BlockSpec((1, l, 2 * KV_WIDTH), lambda i, j: (i, 0, 0)),
        ],
        out_specs=pl.BlockSpec((1, tq, D_ATTN), lambda i, j: (i, j, 0)),
        compiler_params=_params(("arbitrary", "arbitrary"), 32),
        name="window_attn",
    )(sink, q, kd, vd)


def _filter_kernel(z_ref, w1_ref, b1_ref, w2_ref, b2_ref, w3_ref, b3_ref, w4_ref, fr_ref, dec_ref,
                   e_ref, d_ref):
    z = z_ref[...]
    fr = fr_ref[...]
    h = jnp.sin(fr * (_dot3(z, w1_ref[...]) + b1_ref[...]))
    h = jnp.sin(fr * (_dot3(h, w2_ref[...]) + b2_ref[...]))
    h = jnp.sin(fr * (_dot3(h, w3_ref[...]) + b3_ref[...]))
    filt = _dot3(h, w4_ref[...]) * jnp.exp(-z[:, 0:1] * jnp.abs(dec_ref[...]))
    fwd, bwd = filt[:, :D_HY], filt[:, D_HY:]
    e_ref[...] = (fwd + bwd).astype(BF16)
    d_ref[...] = (fwd - bwd).astype(BF16)


def _filter_features(l):
    t = jnp.linspace(0.0, 1.0, l, dtype=F32)[:, None]
    n_bands = (FILTER_EMB - 1) // 2
    w = 2.0 * math.pi * jnp.arange(l, dtype=F32)[:, None] / l
    fb = jnp.linspace(1e-4, n_bands - 1, n_bands, dtype=F32)[None]
    z = jnp.concatenate([t, jnp.cos(fb * w), -jnp.sin(fb * w)], axis=-1)
    return jnp.pad(z, ((0, 0), (0, FILTER_HID - FILTER_EMB)))


def _filters(z, w1, b1, w2, b2, w3, b3, w4, freq, decay):
    l = z.shape[0]
    tl = min(ROW_TILE, l)
    w1p = jnp.pad(w1, ((0, FILTER_HID - FILTER_EMB), (0, 0)))
    vec = lambda a: a.reshape(1, -1)
    out = jax.ShapeDtypeStruct((l, D_HY), BF16)
    tile = pl.BlockSpec((tl, D_HY), lambda i: (i, 0))
    return pl.pallas_call(
        _filter_kernel,
        out_shape=(out, out),
        grid=(l // tl,),
        in_specs=[
            pl.BlockSpec((tl, FILTER_HID), lambda i: (i, 0)),
            _resident((FILTER_HID, FILTER_HID)), _resident((1, FILTER_HID)),
            _resident((FILTER_HID, FILTER_HID)), _resident((1, FILTER_HID)),
            _resident((FILTER_HID, FILTER_HID)), _resident((1, FILTER_HID)),
            _resident((FILTER_HID, 2 * D_HY)), _resident((1, FILTER_HID)), _resident((1, 2 * D_HY)),
        ],
        out_specs=(tile, tile),
        compiler_params=_params(("arbitrary",), 32),
        name="hyena_filter",
    )(z, w1p, vec(b1), w2, vec(b2), w3, vec(b3), w4, vec(freq), vec(decay))


def _spectrum_kernel(m_ref, e_ref, d_ref, ka_ref, kb_ref, ka2_ref, *, seq):
    i = pl.program_id(0)
    tf = ka_ref.shape[0]
    mat = m_ref[0]
    re = _dot(mat[:tf], e_ref[...])
    im = _dot(mat[tf:], d_ref[...])
    nyq = _dot(mat[tf:tf + HALO], e_ref[...])[0:1]
    is_dc = (lax.broadcasted_iota(jnp.int32, (tf, 1), 0) == 0) & (i == 0)
    scale = jnp.where(is_dc, 0.5 / seq, 1.0 / seq)
    ka = re * scale
    ka_ref[...] = ka
    kb_ref[...] = jnp.where(is_dc, 0.0, im * scale)
    ka2_ref[...] = jnp.where(is_dc, nyq * (0.5 / seq), ka)


def _spectrum(mfwd, e, d):
    nf, tf2, l = mfwd.shape
    tf = tf2 // 2
    out = jax.ShapeDtypeStruct((l, D_HY), F32)
    tile = pl.BlockSpec((tf, D_HY), lambda i: (i, 0))
    return pl.pallas_call(
        functools.partial(_spectrum_kernel, seq=l),
        out_shape=(out, out, out),
        grid=(nf,),
        in_specs=[pl.BlockSpec((1, tf2, l), lambda i: (i, 0, 0)), _resident((l, D_HY)), _resident((l, D_HY))],
        out_specs=(tile, tile, tile),
        compiler_params=_params(("arbitrary",), 40),
        name="hyena_spectrum",
    )(mfwd, e, d)


def _dft_matrices(l, tf):
    nf = l // tf
    j = jnp.arange(l, dtype=jnp.int32)

    def trig(f):
        idx = (f[:, None] * j[None, :]) % (2 * l)
        ang = idx.astype(F32) * (math.pi / l)
        return jnp.cos(ang), jnp.sin(ang)

    ac, as_ = trig(jnp.arange(nf, dtype=jnp.int32) * tf)
    bc, bs = trig(jnp.arange(tf, dtype=jnp.int32))
    cosm = ac[:, None, :] * bc[None] - as_[:, None, :] * bs[None]
    sinm = as_[:, None, :] * bc[None] + ac[:, None, :] * bs[None]
    alt = jnp.where(j % 2 == 0, 1.0, -1.0).astype(F32)
    sinm = sinm.at[0, 0, :].set(alt)
    mfwd = jnp.concatenate([cosm, sinm], axis=1).astype(BF16)
    return mfwd, mfwd.reshape(2 * l, l).T


def _dft_fwd_kernel(hz_ref, hzp_ref, hzn_ref, cw_ref, cb_ref, m_ref, ka_ref, kb_ref, ka2_ref,
                    y_ref, p_ref, x0_ref, p_scr, *, nprep):
    s = pl.program_id(1)
    tr = hz_ref.shape[1]
    tf = ka_ref.shape[0]

    @pl.when(s < nprep)
    def _():
        z = hz_ref[0].astype(F32)
        prev = jnp.where(s == 0, 0.0, hzp_ref[0][HALO - 1:HALO].astype(F32))
        nxt = jnp.where(s == nprep - 1, 0.0, hzn_ref[0][0:1].astype(F32))
        row = lax.broadcasted_iota(jnp.int32, (tr, 1), 0)
        zp = jnp.where(row == 0, prev, pltpu.roll(z, 1, 0))
        zn = jnp.where(row == tr - 1, nxt, pltpu.roll(z, tr - 1, 0))
        w = cw_ref[...]
        c = zp * w[0:1] + z * w[1:2] + zn * w[2:3] + cb_ref[...]
        p = c[:, 2 * D_HY:] * c[:, D_HY:2 * D_HY]
        p = p.astype(BF16)
        p_ref[0] = p
        x0_ref[0] = c[:, :D_HY].astype(BF16)
        p_scr[pl.ds(pl.multiple_of(s * tr, tr), tr), :] = p

    @pl.when(s >= nprep)
    def _():
        u = _dot(m_ref[0], p_scr[...])
        ure, uim = u[:tf], u[tf:]
        ka, kb, ka2 = ka_ref[...], kb_ref[...], ka2_ref[...]
        y_ref[0, 0, :tf] = (ure * ka - uim * kb).astype(BF16)
        y_ref[0, 0, tf:] = (ure * kb + uim * ka2).astype(BF16)


def _dft_fwd(hz, conv_w, conv_b, mfwd, ka, kb, ka2):
    b, l, _ = hz.shape
    nf, tf2, _ = mfwd.shape
    tf = tf2 // 2
    tr = min(PREP_TILE, l)
    nprep = l // tr
    per = tr // HALO
    last = l // HALO - 1
    prep = lambda s: jnp.minimum(s, nprep - 1)
    freq = lambda s: jnp.maximum(s - nprep, 0)
    ktile = pl.BlockSpec((tf, D_HY), lambda i, s: (freq(s), 0))
    rtile = pl.BlockSpec((1, tr, D_HY), lambda i, s: (i, prep(s), 0))
    return pl.pallas_call(
        functools.partial(_dft_fwd_kernel, nprep=nprep),
        out_shape=(
            jax.ShapeDtypeStruct((b, nf, tf2, D_HY), BF16),
            jax.ShapeDtypeStruct((b, l, D_HY), BF16),
            jax.ShapeDtypeStruct((b, l, D_HY), BF16),
        ),
        grid=(b, nprep + nf),
        in_specs=[
            pl.BlockSpec((1, tr, 3 * D_HY), lambda i, s: (i, prep(s), 0)),
            pl.BlockSpec((1, HALO, 3 * D_HY), lambda i, s: (i, jnp.maximum(prep(s) * per - 1, 0), 0)),
            pl.BlockSpec((1, HALO, 3 * D_HY), lambda i, s: (i, jnp.minimum((prep(s) + 1) * per, last), 0)),
            _resident((3, 3 * D_HY)),
            _resident((1, 3 * D_HY)),
            pl.BlockSpec((1, tf2, l), lambda i, s: (freq(s), 0, 0)),
            ktile, ktile, ktile,
        ],
        out_specs=(
            pl.BlockSpec((1, 1, tf2, D_HY), lambda i, s: (i, freq(s), 0, 0)),
            rtile, rtile,
        ),
        scratch_shapes=[pltpu.VMEM((l, D_HY), BF16)],
        compiler_params=_params(("arbitrary", "arbitrary"), 48),
        name="hyena_dft_fwd",
    )(hz, hz, hz, conv_w, conv_b.reshape(1, -1), mfwd, ka, kb, ka2)


def _dft_inv_kernel(g_ref, y_ref, p_ref, x0_ref, bias_ref, o_ref):
    y = _dot(g_ref[...], y_ref[0])
    o_ref[0] = ((y + p_ref[0].astype(F32) * bias_ref[...]) * x0_ref[0].astype(F32)).astype(BF16)


def _dft_inv(ginv, y, p, x0, bias):
    b, l, _ = p.shape
    ti = min(TIME_TILE, l)
    tile = pl.BlockSpec((1, ti, D_HY), lambda i, j: (i, j, 0))
    return pl.pallas_call(
        _dft_inv_kernel,
        out_shape=jax.ShapeDtypeStruct((b, l, D_HY), BF16),
        grid=(b, l // ti),
        in_specs=[
            pl.BlockSpec((ti, 2 * l), lambda i, j: (j, 0)),
            pl.BlockSpec((1, 2 * l, D_HY), lambda i, j: (i, 0, 0)),
            tile, tile,
            _resident((1, D_HY)),
        ],
        out_specs=tile,
        compiler_params=_params(("arbitrary", "arbitrary"), 48),
        name="hyena_dft_inv",
    )(ginv, y, p, x0, bias.reshape(1, -1))


def _out_proj_kernel(x_ref, mod_ref, oa_ref, oh_ref, w_ref, gn_ref, g_ref, b_ref, o_ref):
    x = x_ref[0]
    gate = mod_ref[0][5:6]
    gn = gn_ref[...]
    a = _rms_norm(oa_ref[0].astype(F32), gn[:, :D_ATTN]).astype(BF16)
    h = _rms_norm(oh_ref[0].astype(F32), gn[:, D_ATTN:]).astype(BF16)
    o = _dot(a, w_ref[:D_ATTN, :]) + _dot(h, w_ref[D_ATTN:, :])
    o_ref[0] = _layer_norm(ALPHA * x + (1.0 + gate) * o, g_ref[...], b_ref[...])


def _out_proj(x, mod, o_attn, o_hy, w_out, gn, ln_g, ln_b):
    b, l, _ = x.shape
    tm = min(ROW_TILE, l)
    row = lambda i, j: (i, j, 0)
    return pl.pallas_call(
        _out_proj_kernel,
        out_shape=jax.ShapeDtypeStruct(x.shape, F32),
        grid=(b, l // tm),
        in_specs=[
            pl.BlockSpec((1, tm, D_MODEL), row),
            pl.BlockSpec((1, N_MOD, D_MODEL), lambda i, j: (i, 0, 0)),
            pl.BlockSpec((1, tm, D_ATTN), row),
            pl.BlockSpec((1, tm, D_HY), row),
            _resident((D_MODEL, D_MODEL)),
            _resident((1, D_MODEL)), _resident((1, D_MODEL)), _resident((1, D_MODEL)),
        ],
        out_specs=pl.BlockSpec((1, tm, D_MODEL), row),
        compiler_params=_params(("arbitrary", "arbitrary"), 40),
        name="out_proj",
    )(x, mod, o_attn, o_hy, w_out, gn, ln_g, ln_b)


def _chunk_ffn_weights(wi, wo):
    nck = D_FF // FF_CHUNK
    g = wi[:, :D_FF].reshape(D_MODEL, nck, FF_CHUNK)
    u = wi[:, D_FF:].reshape(D_MODEL, nck, FF_CHUNK)
    wi_c = jnp.concatenate([g, u], axis=2).transpose(1, 0, 2).astype(BF16)
    return wi_c, wo.astype(BF16)


def _trunk(x, mod, weights, p):
    l = x.shape[1]
    depth = mod.shape[0]
    rope = _rope_tables(l)
    feats = _filter_features(l)
    mfwd, ginv = _dft_matrices(l, min(FREQ_TILE, l))
    vec = lambda a: a.reshape(1, -1)
    for i in range(depth):
        w = weights[i]
        m = mod[i]
        x = _ffn(x, m, w["wi1"], w["wo1"], vec(p["ln_g"][i, 0]), vec(p["ln_b"][i, 0]), 0)
        q, kd, vd, hz = _in_proj(x, m, w["w_in"], rope)
        o_attn = _attention(q, kd, vd, p["sink"][i])
        e, d = _filters(feats, p["hy_w1"][i], p["hy_b1"][i], p["hy_w2"][i], p["hy_b2"][i],
                        p["hy_w3"][i], p["hy_b3"][i], p["hy_w4"][i], p["hy_freq"][i], p["hy_decay"][i])
        ka, kb, ka2 = _spectrum(mfwd, e, d)
        y, gated, x0 = _dft_fwd(hz, p["hy_conv_w"][i], p["hy_conv_b"][i], mfwd, ka, kb, ka2)
        o_hy = _dft_inv(ginv, y.reshape(y.shape[0], 2 * l, D_HY), gated, x0, p["hy_bias"][i])
        x = _out_proj(x, m, o_attn, o_hy, w["w_out"], vec(p["grp_norm_g"][i]),
                      vec(p["ln_g"][i, 1]), vec(p["ln_b"][i, 1]))
        x = _ffn(x, m, w["wi2"], w["wo2"], vec(p["ln_g"][i, 2]), vec(p["ln_b"][i, 2]), 6)
    return x


def kernel(x_prompt, x_sample, c_prompt, c_sample, ada_w, ada_b, ffn1_wi, ffn1_wo, ffn2_wi, ffn2_wo, ln_g, ln_b, w_in, w_out, sink, grp_norm_g, hy_conv_w, hy_conv_b, hy_w1, hy_b1, hy_w2, hy_b2, hy_w3, hy_b3, hy_w4, hy_freq, hy_decay, hy_bias):
    p = dict(ln_g=ln_g, ln_b=ln_b, sink=sink, grp_norm_g=grp_norm_g, hy_conv_w=hy_conv_w,
             hy_conv_b=hy_conv_b, hy_w1=hy_w1, hy_b1=hy_b1, hy_w2=hy_w2, hy_b2=hy_b2, hy_w3=hy_w3,
             hy_b3=hy_b3, hy_w4=hy_w4, hy_freq=hy_freq, hy_decay=hy_decay, hy_bias=hy_bias)
    depth = ada_w.shape[0]
    weights = []
    for i in range(depth):
        wi1, wo1 = _chunk_ffn_weights(ffn1_wi[i], ffn1_wo[i])
        wi2, wo2 = _chunk_ffn_weights(ffn2_wi[i], ffn2_wo[i])
        weights.append(dict(wi1=wi1, wo1=wo1, wi2=wi2, wo2=wo2,
                            w_in=w_in[i].astype(BF16), w_out=w_out[i].astype(BF16)))
    nb = c_prompt.shape[0]
    mod = _modulation(jnp.concatenate([c_prompt, c_sample], axis=0), ada_w, ada_b)
    y_prompt = _trunk(x_prompt, mod[:, :nb], weights, p)
    y_sample = _trunk(x_sample, mod[:, nb:], weights, p)
    return (y_prompt, y_sample)
```

```python
import functools
import math

import jax
import jax.numpy as jnp
from jax import lax
from jax.experimental import pallas as pl
from jax.experimental.pallas import tpu as pltpu

F32 = jnp.float32
BF16 = jnp.bfloat16

D_MODEL = 1024
D_ATTN = 512
D_HY = 512
HEAD_DIM = 64
N_Q_HEADS = D_ATTN // HEAD_DIM
N_KV_HEADS = 2
KV_WIDTH = N_KV_HEADS * HEAD_DIM
ROT_DIM = HEAD_DIM // 4
ROPE_THETA = 500000.0
WINDOW = 128
FILTER_EMB = 33
FILTER_HID = 64
D_FF = 2816
N_IN = D_ATTN + 2 * KV_WIDTH + 3 * D_HY
N_MOD = 9
DEPTH = 4
ALPHA = float((2 * DEPTH) ** 0.25)
LN_EPS = 1e-5
RMS_EPS = 1e-6
NEG_BIG = -1e30
LOG2E = math.log2(math.e)
Q_SCALE = HEAD_DIM ** -0.5 * LOG2E

LANES = 128
VMEM_BYTES = 64 * 1024 * 1024
MIB = 1024 * 1024

ROW_TILE = 512
FFN_TILE = 1024
FFN_SUB = 512
FF_CHUNK = 256
Q_TILE = 512
Q_BLOCK = 128
BAND = 3 * Q_BLOCK
FFT_N2 = 256
FFT_LANES = 256
ROW_GROUP = 16
FFT_UNROLL = 8
PREP_TILE = 512
MOD_N_TILE = 1536
HALO = 16


def _params(semantics, vmem_mib):
    return pltpu.CompilerParams(dimension_semantics=semantics,
                                vmem_limit_bytes=min(vmem_mib * MIB, VMEM_BYTES - 4 * MIB))


def _resident(shape):
    nd = len(shape)
    return pl.BlockSpec(shape, lambda *_: (0,) * nd, pipeline_mode=pl.Buffered(1))


def _dot(a, b):
    return jnp.dot(a, b, preferred_element_type=F32)


def _split(a):
    hi = a.astype(BF16)
    lo = (a - hi.astype(F32)).astype(BF16)
    return hi, lo


def _dot3(a, b):
    a_hi, a_lo = _split(a)
    b_hi, b_lo = _split(b)
    return _dot(a_hi, b_hi) + _dot(a_hi, b_lo) + _dot(a_lo, b_hi)


def _layer_norm(y, g, b):
    mu = jnp.mean(y, axis=-1, keepdims=True)
    d = y - mu
    var = jnp.mean(d * d, axis=-1, keepdims=True)
    return d * lax.rsqrt(var + LN_EPS) * g + b


def _rms_norm(y, g):
    ms = jnp.mean(y * y, axis=-1, keepdims=True)
    return y * lax.rsqrt(ms + RMS_EPS) * g


def _mod_kernel(c_ref, w_ref, b_ref, o_ref):
    c = c_ref[...]
    a = c * jax.nn.sigmoid(c)
    o_ref[0] = _dot3(a, w_ref[0]) + b_ref[0]


def _modulation(c, ada_w, ada_b):
    depth = ada_w.shape[0]
    b = c.shape[0]
    n = N_MOD * D_MODEL
    out = pl.pallas_call(
        _mod_kernel,
        out_shape=jax.ShapeDtypeStruct((depth, b, n), F32),
        grid=(depth, n // MOD_N_TILE),
        in_specs=[
            pl.BlockSpec((b, D_MODEL), lambda l, j: (0, 0)),
            pl.BlockSpec((1, D_MODEL, MOD_N_TILE), lambda l, j: (l, 0, j)),
            pl.BlockSpec((1, 1, MOD_N_TILE), lambda l, j: (l, 0, j)),
        ],
        out_specs=pl.BlockSpec((1, b, MOD_N_TILE), lambda l, j: (l, 0, j)),
        compiler_params=_params(("arbitrary", "arbitrary"), 40),
        name="adaln_mod",
    )(c, ada_w, ada_b.reshape(depth, 1, n))
    return out.reshape(depth, b, N_MOD, D_MODEL)


def _ffn_kernel(x_ref, mod_ref, wi_ref, wo_ref, g_ref, b_ref, o_ref, h_scr, a_scr, *, mod_row):
    m = mod_ref[0]
    shift, scale, gate = m[mod_row:mod_row + 1], m[mod_row + 1:mod_row + 2], m[mod_row + 2:mod_row + 3]
    half_gate = 0.5 * (1.0 + gate)
    for s in range(h_scr.shape[0]):
        rows = slice(s * FFN_SUB, (s + 1) * FFN_SUB)
        h_scr[s] = (x_ref[0, rows, :] * (1.0 + scale) + shift).astype(BF16)
        for c in range(D_FF // FF_CHUNK):
            gu = _dot(h_scr[s], wi_ref[c])
            g, u = gu[:, :FF_CHUNK], gu[:, FF_CHUNK:]
            a_scr[s, :, c * FF_CHUNK:(c + 1) * FF_CHUNK] = (g * jax.nn.sigmoid(g) * u).astype(BF16)
        y = ALPHA * x_ref[0, rows, :] + half_gate * _dot(a_scr[s], wo_ref[...])
        o_ref[0, rows, :] = _layer_norm(y, g_ref[...], b_ref[...])


def _ffn(x, mod, wi_c, wo, ln_g, ln_b, mod_row):
    b, l, _ = x.shape
    tm = min(FFN_TILE, l)
    nsub = tm // FFN_SUB
    nck = D_FF // FF_CHUNK
    return pl.pallas_call(
        functools.partial(_ffn_kernel, mod_row=mod_row),
        out_shape=jax.ShapeDtypeStruct(x.shape, F32),
        grid=(b, l // tm),
        in_specs=[
            pl.BlockSpec((1, tm, D_MODEL), lambda i, j: (i, j, 0)),
            pl.BlockSpec((1, N_MOD, D_MODEL), lambda i, j: (i, 0, 0)),
            _resident((nck, D_MODEL, 2 * FF_CHUNK)),
            _resident((D_FF, D_MODEL)),
            _resident((1, D_MODEL)),
            _resident((1, D_MODEL)),
        ],
        out_specs=pl.BlockSpec((1, tm, D_MODEL), lambda i, j: (i, j, 0)),
        scratch_shapes=[pltpu.VMEM((nsub, FFN_SUB, D_MODEL), BF16), pltpu.VMEM((nsub, FFN_SUB, D_FF), BF16)],
        compiler_params=_params(("arbitrary", "arbitrary"), 56),
        name="ffn",
    )(x, mod, wi_c, wo, ln_g, ln_b)


def _rope_slab(t, cos, sin_up, sin_dn):
    return (t * cos + pltpu.roll(t, ROT_DIM // 2, 1) * sin_up
            + pltpu.roll(t, LANES - ROT_DIM // 2, 1) * sin_dn)


def _in_proj_kernel(x_ref, mod_ref, w_ref, cos_ref, sup_ref, sdn_ref,
                    q_ref, kd_ref, vd_ref, hz_ref):
    x = x_ref[0]
    m = mod_ref[0]
    h = (x * (1.0 + m[4:5]) + m[3:4]).astype(BF16)
    z = _dot(h, w_ref[...])
    cos, sup, sdn = cos_ref[...], sup_ref[...], sdn_ref[...]
    for s in range(D_ATTN // LANES):
        q = _rope_slab(z[:, s * LANES:(s + 1) * LANES], cos, sup, sdn)
        q_ref[0, :, s * LANES:(s + 1) * LANES] = (q * Q_SCALE).astype(BF16)
    k = _rope_slab(z[:, D_ATTN:D_ATTN + KV_WIDTH], cos, sup, sdn)
    v = z[:, D_ATTN + KV_WIDTH:D_ATTN + 2 * KV_WIDTH]
    lane = lax.broadcasted_iota(jnp.int32, k.shape, 1)
    first = lane < HEAD_DIM
    for t, ref in ((k, kd_ref), (v, vd_ref)):
        swapped = pltpu.roll(t, HEAD_DIM, 1)
        ref[0, :, :LANES] = jnp.where(first, t, swapped).astype(BF16)
        ref[0, :, LANES:] = jnp.where(first, swapped, t).astype(BF16)
    hz_ref[0] = z[:, D_ATTN + 2 * KV_WIDTH:].astype(BF16)


def _in_proj(x, mod, w_in, rope):
    b, l, _ = x.shape
    tm = min(ROW_TILE, l)
    row = lambda i, j: (i, j, 0)
    tab = pl.BlockSpec((tm, LANES), lambda i, j: (j, 0))
    return pl.pallas_call(
        _in_proj_kernel,
        out_shape=(
            jax.ShapeDtypeStruct((b, l, D_ATTN), BF16),
            jax.ShapeDtypeStruct((b, l, 2 * KV_WIDTH), BF16),
            jax.ShapeDtypeStruct((b, l, 2 * KV_WIDTH), BF16),
            jax.ShapeDtypeStruct((b, l, 3 * D_HY), BF16),
        ),
        grid=(b, l // tm),
        in_specs=[
            pl.BlockSpec((1, tm, D_MODEL), row),
            pl.BlockSpec((1, N_MOD, D_MODEL), lambda i, j: (i, 0, 0)),
            _resident((D_MODEL, N_IN)),
            tab, tab, tab,
        ],
        out_specs=(
            pl.BlockSpec((1, tm, D_ATTN), row),
            pl.BlockSpec((1, tm, 2 * KV_WIDTH), row),
            pl.BlockSpec((1, tm, 2 * KV_WIDTH), row),
            pl.BlockSpec((1, tm, 3 * D_HY), row),
        ),
        compiler_params=_params(("arbitrary", "arbitrary"), 48),
        name="in_proj",
    )(x, mod, w_in, *rope)


def _rope_tables(l):
    half = ROT_DIM // 2
    inv = ROPE_THETA ** (-jnp.arange(0, ROT_DIM, 2, dtype=F32) / ROT_DIM)
    ang = jnp.arange(l, dtype=F32)[:, None] * inv[None]
    cos, sin = jnp.cos(ang), jnp.sin(ang)
    rest = HEAD_DIM - ROT_DIM
    cos_h = jnp.concatenate([cos, cos, jnp.ones((l, rest), F32)], axis=1)
    up_h = jnp.concatenate([jnp.zeros((l, half), F32), sin, jnp.zeros((l, rest), F32)], axis=1)
    dn_h = jnp.concatenate([-sin, jnp.zeros((l, half + rest), F32)], axis=1)
    reps = LANES // HEAD_DIM
    return tuple(jnp.tile(t, (1, reps)) for t in (cos_h, up_h, dn_h))


def _attn_kernel(sink_ref, q_ref, kd_ref, vd_ref, o_ref, *, seq):
    j = pl.program_id(1)
    first = lax.broadcasted_iota(jnp.int32, (BAND, LANES), 1) < HEAD_DIM
    out_first = lax.broadcasted_iota(jnp.int32, (Q_BLOCK, LANES), 1) < HEAD_DIM
    zero = jnp.zeros((BAND, LANES), BF16)
    ones_ext = jnp.concatenate([jnp.where(first, 1.0, 0.0), jnp.where(first, 0.0, 1.0)], axis=0).astype(BF16)
    for blk in range(q_ref.shape[1] // Q_BLOCK):
        q0 = j * q_ref.shape[1] + blk * Q_BLOCK
        start = pl.multiple_of(jnp.clip(q0 - Q_BLOCK, 0, seq - BAND), Q_BLOCK)
        qpos = q0 + lax.broadcasted_iota(jnp.int32, (Q_BLOCK, BAND), 0)
        kpos = start + lax.broadcasted_iota(jnp.int32, (Q_BLOCK, BAND), 1)
        bias = jnp.where(jnp.abs(qpos - kpos) <= WINDOW, 0.0, NEG_BIG).astype(F32)
        rows = slice(blk * Q_BLOCK, (blk + 1) * Q_BLOCK)
        for kh in range(N_KV_HEADS):
            cols = slice(kh * LANES, (kh + 1) * LANES)
            kslab = kd_ref[0, pl.ds(start, BAND), cols]
            vslab = vd_ref[0, pl.ds(start, BAND), cols]
            kb = jnp.concatenate([jnp.where(first, kslab, zero), jnp.where(first, zero, kslab)], axis=0)
            vb = jnp.concatenate([jnp.where(first, vslab, zero), jnp.where(first, zero, vslab)], axis=0)
            vb_ext = jnp.concatenate([vb, ones_ext], axis=1)
            q = jnp.concatenate([q_ref[0, rows, (2 * kh + r) * LANES:(2 * kh + r + 1) * LANES] for r in range(2)],
                                axis=0)
            s = lax.dot_general(q, kb, (((1,), (1,)), ((), ())), preferred_element_type=F32)
            probs, sink_terms = [], []
            for r in range(2):
                pr, sk = [], []
                for e in range(2):
                    sink = sink_ref[4 * kh + 2 * r + e] * LOG2E
                    se = s[r * Q_BLOCK:(r + 1) * Q_BLOCK, e * BAND:(e + 1) * BAND] + bias
                    mx = jnp.maximum(jnp.max(se, axis=1, keepdims=True), sink)
                    pr.append(jnp.exp2(se - mx).astype(BF16))
                    sk.append(jnp.exp2(sink - mx))
                probs.append(jnp.concatenate(pr, axis=1))
                sink_terms.append(jnp.where(out_first, sk[0], sk[1]))
            o = _dot(jnp.concatenate(probs, axis=0), vb_ext)
            for r in range(2):
                part = o[r * Q_BLOCK:(r + 1) * Q_BLOCK]
                den = part[:, LANES:] + sink_terms[r]
                o_ref[0, rows, (2 * kh + r) * LANES:(2 * kh + r + 1) * LANES] = (
                    part[:, :LANES] / den).astype(o_ref.dtype)


def _attention(q, kd, vd, sink):
    b, l, _ = q.shape
    tq = min(Q_TILE, l)
    return pl.pallas_call(
        functools.partial(_attn_kernel, seq=l),
        out_shape=jax.ShapeDtypeStruct((b, l, D_ATTN), BF16),
        grid=(b, l // tq),
        in_specs=[
            pl.BlockSpec(memory_space=pltpu.SMEM),
            pl.BlockSpec((1, tq, D_ATTN), lambda i, j: (i, j, 0)),
            pl.BlockSpec((1, l, 2 * KV_WIDTH), lambda i, j: (i, 0, 0)),
            pl.BlockSpec((1, l, 2 * KV_WIDTH), lambda i, j: (i, 0, 0)),
        ],
        out_specs=pl.BlockSpec((1, tq, D_ATTN), lambda i, j: (i, j, 0)),
        compiler_params=_params(("arbitrary", "arbitrary"), 32),
        name="window_attn",
    )(sink, q, kd, vd)


def _filter_kernel(z_ref, w1_ref, b1_ref, w2_ref, b2_ref, w3_ref, b3_ref, w4_ref, fr_ref, dec_ref,
                   e_ref, d_ref):
    z = z_ref[...]
    fr = fr_ref[...]
    h = jnp.sin(fr * (_dot3(z, w1_ref[...]) + b1_ref[...]))
    h = jnp.sin(fr * (_dot3(h, w2_ref[...]) + b2_ref[...]))
    h = jnp.sin(fr * (_dot3(h, w3_ref[...]) + b3_ref[...]))
    filt = _dot3(h, w4_ref[...]) * jnp.exp(-z[:, 0:1] * jnp.abs(dec_ref[...]))
    fwd, bwd = filt[:, :D_HY], filt[:, D_HY:]
    e_ref[...] = (fwd + bwd).astype(BF16)
    d_ref[...] = (fwd - bwd).astype(BF16)


def _filter_features(l):
    t = jnp.linspace(0.0, 1.0, l, dtype=F32)[:, None]
    n_bands = (FILTER_EMB - 1) // 2
    w = 2.0 * math.pi * jnp.arange(l, dtype=F32)[:, None] / l
    fb = jnp.linspace(1e-4, n_bands - 1, n_bands, dtype=F32)[None]
    z = jnp.concatenate([t, jnp.cos(fb * w), -jnp.sin(fb * w)], axis=-1)
    return jnp.pad(z, ((0, 0), (0, FILTER_HID - FILTER_EMB)))


def _filters(z, w1, b1, w2, b2, w3, b3, w4, freq, decay):
    l = z.shape[0]
    tl = min(ROW_TILE, l)
    w1p = jnp.pad(w1, ((0, FILTER_HID - FILTER_EMB), (0, 0)))
    vec = lambda a: a.reshape(1, -1)
    out = jax.ShapeDtypeStruct((l, D_HY), BF16)
    tile = pl.BlockSpec((tl, D_HY), lambda i: (i, 0))
    return pl.pallas_call(
        _filter_kernel,
        out_shape=(out, out),
        grid=(l // tl,),
        in_specs=[
            pl.BlockSpec((tl, FILTER_HID), lambda i: (i, 0)),
            _resident((FILTER_HID, FILTER_HID)), _resident((1, FILTER_HID)),
            _resident((FILTER_HID, FILTER_HID)), _resident((1, FILTER_HID)),
            _resident((FILTER_HID, FILTER_HID)), _resident((1, FILTER_HID)),
            _resident((FILTER_HID, 2 * D_HY)), _resident((1, FILTER_HID)), _resident((1, 2 * D_HY)),
        ],
        out_specs=(tile, tile),
        compiler_params=_params(("arbitrary",), 32),
        name="hyena_filter",
    )(z, w1p, vec(b1), w2, vec(b2), w3, vec(b3), w4, vec(freq), vec(decay))


def _fft_matrices(l):
    n = 2 * l
    n2 = FFT_N2
    n1 = n // n2
    k1 = jnp.arange(n1, dtype=jnp.int32)
    t1 = jnp.arange(n1 // 2, dtype=jnp.int32)
    ang1 = ((k1[:, None] * t1[None, :]) % n1).astype(F32) * (2.0 * math.pi / n1)
    f1 = jnp.stack([jnp.cos(ang1), jnp.sin(ang1)], axis=1).reshape(2 * n1, n1 // 2)
    eye = jnp.eye(ROW_GROUP, dtype=F32)
    kron = (f1[:, None, :, None] * eye[None, :, None, :]).reshape(2 * n1 * ROW_GROUP, (n1 // 2) * ROW_GROUP)
    k2 = jnp.arange(n2 // 2, dtype=jnp.int32)
    t2 = jnp.arange(n2, dtype=jnp.int32)
    k = k1[:, None] + n1 * k2[None, :]
    ang2 = ((k[:, :, None] * t2[None, None, :]) % n).astype(F32) * (2.0 * math.pi / n)
    c, s = jnp.cos(ang2), jnp.sin(ang2)
    re_rows = jnp.concatenate([c, -s], axis=2)
    im_rows = jnp.concatenate([s, c], axis=2)
    nyq = jnp.concatenate([jnp.where(t2 % 2 == 0, 1.0, -1.0).astype(F32), jnp.zeros((n2,), F32)])
    im_rows = im_rows.at[0, 0, :].set(nyq)
    g = jnp.concatenate([re_rows, im_rows], axis=1)
    return dict(kron=kron.astype(BF16), kron_t=kron.T.astype(BF16),
                g=g.astype(BF16), g_t=g.transpose(0, 2, 1).astype(BF16))


def _fft_stage1(x_ref, kron_ref, b_scr):
    n1h, n2, c = x_ref.shape

    def group(g, carry):
        grp = _row_group(g)
        x = x_ref[:, grp, :].reshape(n1h * ROW_GROUP, c)
        b_scr[:, grp, :] = _dot(kron_ref[...], x).astype(BF16).reshape(4 * n1h, ROW_GROUP, c)
        return carry

    lax.fori_loop(0, n2 // ROW_GROUP, group, 0, unroll=FFT_UNROLL)


def _row_group(g):
    return pl.ds(pl.multiple_of(g * ROW_GROUP, ROW_GROUP), ROW_GROUP)


def _fft_stage2(g_ref, b_scr, k):
    n2, c = b_scr.shape[1:]
    return _dot(g_ref[k], b_scr[pl.ds(2 * k, 2)].reshape(2 * n2, c))


def _spectrum_kernel(e_ref, d_ref, kron_ref, g_ref, ka_ref, kb_ref, ka2_ref, be_scr, bd_scr, *, seq):
    half = ka_ref.shape[1]
    _fft_stage1(e_ref, kron_ref, be_scr)
    _fft_stage1(d_ref, kron_ref, bd_scr)
    first_row = lax.broadcasted_iota(jnp.int32, (half, 1), 0) == 0

    def freq(k, carry):
        ue = _fft_stage2(g_ref, be_scr, k)
        ud = _fft_stage2(g_ref, bd_scr, k)
        is_dc = first_row & (k == 0)
        ka = ue[:half] * jnp.where(is_dc, 0.5 / seq, 1.0 / seq)
        ka_ref[k] = ka
        kb_ref[k] = jnp.where(is_dc, 0.0, ud[half:] * (1.0 / seq))
        ka2_ref[k] = jnp.where(is_dc, ue[half:half + 1] * (0.5 / seq), ka)
        return carry

    lax.fori_loop(0, ka_ref.shape[0], freq, 0)


def _spectrum(mats, e, d):
    l = e.shape[0]
    n1, n2 = mats["g"].shape[:2]
    cw = LANES
    out = jax.ShapeDtypeStruct((n1, n2 // 2, D_HY), F32)
    src = pl.BlockSpec((n1 // 2, n2, cw), lambda h: (0, 0, h))
    dst = pl.BlockSpec((n1, n2 // 2, cw), lambda h: (0, 0, h))
    view = lambda a: a.reshape(n1 // 2, n2, D_HY)
    return pl.pallas_call(
        functools.partial(_spectrum_kernel, seq=l),
        out_shape=(out, out, out),
        grid=(D_HY // cw,),
        in_specs=[src, src, _resident(mats["kron"].shape), _resident(mats["g"].shape)],
        out_specs=(dst, dst, dst),
        scratch_shapes=[pltpu.VMEM((2 * n1, n2, cw), BF16), pltpu.VMEM((2 * n1, n2, cw), BF16)],
        compiler_params=_params(("arbitrary",), 56),
        name="hyena_spectrum",
    )(view(e), view(d), mats["kron"], mats["g"])


def _conv_gate_kernel(hz_ref, hzp_ref, hzn_ref, cw_ref, cb_ref, p_ref, x0_ref):
    s = pl.program_id(1)
    tr = hz_ref.shape[1]
    z = hz_ref[0].astype(F32)
    prev = jnp.where(s == 0, 0.0, hzp_ref[0][HALO - 1:HALO].astype(F32))
    nxt = jnp.where(s == pl.num_programs(1) - 1, 0.0, hzn_ref[0][0:1].astype(F32))
    row = lax.broadcasted_iota(jnp.int32, (tr, 1), 0)
    zp = jnp.where(row == 0, prev, pltpu.roll(z, 1, 0))
    zn = jnp.where(row == tr - 1, nxt, pltpu.roll(z, tr - 1, 0))
    w = cw_ref[...]
    c = zp * w[0:1] + z * w[1:2] + zn * w[2:3] + cb_ref[...]
    p_ref[0] = (c[:, 2 * D_HY:] * c[:, D_HY:2 * D_HY]).astype(BF16)
    x0_ref[0] = c[:, :D_HY].astype(BF16)


def _conv_gate(hz, conv_w, conv_b):
    b, l, _ = hz.shape
    tr = min(PREP_TILE, l)
    per = tr // HALO
    last = l // HALO - 1
    out = jax.ShapeDtypeStruct((b, l, D_HY), BF16)
    tile = pl.BlockSpec((1, tr, D_HY), lambda i, s: (i, s, 0))
    return pl.pallas_call(
        _conv_gate_kernel,
        out_shape=(out, out),
        grid=(b, l // tr),
        in_specs=[
            pl.BlockSpec((1, tr, 3 * D_HY), lambda i, s: (i, s, 0)),
            pl.BlockSpec((1, HALO, 3 * D_HY), lambda i, s: (i, jnp.maximum(s * per - 1, 0), 0)),
            pl.BlockSpec((1, HALO, 3 * D_HY), lambda i, s: (i, jnp.minimum((s + 1) * per, last), 0)),
            _resident((3, 3 * D_HY)),
            _resident((1, 3 * D_HY)),
        ],
        out_specs=(tile, tile),
        compiler_params=_params(("arbitrary", "arbitrary"), 32),
        name="hyena_conv_gate",
    )(hz, hz, hz, conv_w, conv_b.reshape(1, -1))


def _long_conv_kernel(p_ref, x0_ref, kron_ref, kron_t_ref, g_ref, gt_ref, ka_ref, kb_ref, ka2_ref, bias_ref,
                      o_ref, b_scr):
    n1h, n2, c = p_ref.shape[1:]
    half = n2 // 2
    _fft_stage1(p_ref.at[0], kron_ref, b_scr)

    def freq(k, carry):
        u = _fft_stage2(g_ref, b_scr, k)
        ure, uim = u[:half], u[half:]
        ka, kb, ka2 = ka_ref[k], kb_ref[k], ka2_ref[k]
        y = jnp.concatenate([ure * ka - uim * kb, ure * kb + uim * ka2], axis=0).astype(BF16)
        b_scr[pl.ds(2 * k, 2)] = _dot(gt_ref[k], y).astype(BF16).reshape(2, n2, c)
        return carry

    lax.fori_loop(0, 2 * n1h, freq, 0, unroll=FFT_UNROLL)
    bias = bias_ref[...]

    def group(g, carry):
        grp = _row_group(g)
        z = b_scr[:, grp, :].reshape(4 * n1h * ROW_GROUP, c)
        y = _dot(kron_t_ref[...], z).reshape(n1h, ROW_GROUP, c)
        gated = (y + p_ref[0, :, grp, :].astype(F32) * bias) * x0_ref[0, :, grp, :].astype(F32)
        o_ref[0, :, grp, :] = gated.astype(BF16)
        return carry

    lax.fori_loop(0, n2 // ROW_GROUP, group, 0, unroll=FFT_UNROLL)


def _long_conv(p, x0, mats, ka, kb, ka2, bias):
    b, l, _ = p.shape
    n1, n2 = mats["g"].shape[:2]
    cw = FFT_LANES
    view = lambda a: a.reshape(b, n1 // 2, n2, D_HY)
    tile = pl.BlockSpec((1, n1 // 2, n2, cw), lambda h, i: (i, 0, 0, h))
    spec = pl.BlockSpec((n1, n2 // 2, cw), lambda h, i: (0, 0, h), pipeline_mode=pl.Buffered(1))
    out = pl.pallas_call(
        _long_conv_kernel,
        out_shape=jax.ShapeDtypeStruct((b, n1 // 2, n2, D_HY), BF16),
        grid=(D_HY // cw, b),
        in_specs=[
            tile, tile,
            _resident(mats["kron"].shape), _resident(mats["kron_t"].shape),
            _resident(mats["g"].shape), _resident(mats["g_t"].shape),
            spec, spec, spec,
            pl.BlockSpec((1, cw), lambda h, i: (0, h)),
        ],
        out_specs=tile,
        scratch_shapes=[pltpu.VMEM((2 * n1, n2, cw), BF16)],
        compiler_params=_params(("arbitrary", "arbitrary"), 60),
        name="hyena_long_conv",
    )(view(p), view(x0), mats["kron"], mats["kron_t"], mats["g"], mats["g_t"], ka, kb, ka2, bias.reshape(1, -1))
    return out.reshape(b, l, D_HY)


def _out_proj_kernel(x_ref, mod_ref, oa_ref, oh_ref, w_ref, gn_ref, g_ref, b_ref, o_ref):
    x = x_ref[0]
    gate = mod_ref[0][5:6]
    gn = gn_ref[...]
    a = _rms_norm(oa_ref[0].astype(F32), gn[:, :D_ATTN]).astype(BF16)
    h = _rms_norm(oh_ref[0].astype(F32), gn[:, D_ATTN:]).astype(BF16)
    o = _dot(a, w_ref[:D_ATTN, :]) + _dot(h, w_ref[D_ATTN:, :])
    o_ref[0] = _layer_norm(ALPHA * x + (1.0 + gate) * o, g_ref[...], b_ref[...])


def _out_proj(x, mod, o_attn, o_hy, w_out, gn, ln_g, ln_b):
    b, l, _ = x.shape
    tm = min(ROW_TILE, l)
    row = lambda i, j: (i, j, 0)
    return pl.pallas_call(
        _out_proj_kernel,
        out_shape=jax.ShapeDtypeStruct(x.shape, F32),
        grid=(b, l // tm),
        in_specs=[
            pl.BlockSpec((1, tm, D_MODEL), row),
            pl.BlockSpec((1, N_MOD, D_MODEL), lambda i, j: (i, 0, 0)),
            pl.BlockSpec((1, tm, D_ATTN), row),
            pl.BlockSpec((1, tm, D_HY), row),
            _resident((D_MODEL, D_MODEL)),
            _resident((1, D_MODEL)), _resident((1, D_MODEL)), _resident((1, D_MODEL)),
        ],
        out_specs=pl.BlockSpec((1, tm, D_MODEL), row),
        compiler_params=_params(("arbitrary", "arbitrary"), 40),
        name="out_proj",
    )(x, mod, o_attn, o_hy, w_out, gn, ln_g, ln_b)


def _chunk_ffn_weights(wi, wo):
    nck = D_FF // FF_CHUNK
    g = wi[:, :D_FF].reshape(D_MODEL, nck, FF_CHUNK)
    u = wi[:, D_FF:].reshape(D_MODEL, nck, FF_CHUNK)
    wi_c = jnp.concatenate([g, u], axis=2).transpose(1, 0, 2).astype(BF16)
    return wi_c, wo.astype(BF16)


def _trunk(x, mod, weights, p):
    l = x.shape[1]
    depth = mod.shape[0]
    rope = _rope_tables(l)
    feats = _filter_features(l)
    mats = _fft_matrices(l)
    vec = lambda a: a.reshape(1, -1)
    for i in range(depth):
        w = weights[i]
        m = mod[i]
        x = _ffn(x, m, w["wi1"], w["wo1"], vec(p["ln_g"][i, 0]), vec(p["ln_b"][i, 0]), 0)
        q, kd, vd, hz = _in_proj(x, m, w["w_in"], rope)
        o_attn = _attention(q, kd, vd, p["sink"][i])
        e, d = _filters(feats, p["hy_w1"][i], p["hy_b1"][i], p["hy_w2"][i], p["hy_b2"][i],
                        p["hy_w3"][i], p["hy_b3"][i], p["hy_w4"][i], p["hy_freq"][i], p["hy_decay"][i])
        ka, kb, ka2 = _spectrum(mats, e, d)
        gated, x0 = _conv_gate(hz, p["hy_conv_w"][i], p["hy_conv_b"][i])
        o_hy = _long_conv(gated, x0, mats, ka, kb, ka2, p["hy_bias"][i])
        x = _out_proj(x, m, o_attn, o_hy, w["w_out"], vec(p["grp_norm_g"][i]),
                      vec(p["ln_g"][i, 1]), vec(p["ln_b"][i, 1]))
        x = _ffn(x, m, w["wi2"], w["wo2"], vec(p["ln_g"][i, 2]), vec(p["ln_b"][i, 2]), 6)
    return x


def kernel(x_prompt, x_sample, c_prompt, c_sample, ada_w, ada_b, ffn1_wi, ffn1_wo, ffn2_wi, ffn2_wo, ln_g, ln_b, w_in, w_out, sink, grp_norm_g, hy_conv_w, hy_conv_b, hy_w1, hy_b1, hy_w2, hy_b2, hy_w3, hy_b3, hy_w4, hy_freq, hy_decay, hy_bias):
    p = dict(ln_g=ln_g, ln_b=ln_b, sink=sink, grp_norm_g=grp_norm_g, hy_conv_w=hy_conv_w,
             hy_conv_b=hy_conv_b, hy_w1=hy_w1, hy_b1=hy_b1, hy_w2=hy_w2, hy_b2=hy_b2, hy_w3=hy_w3,
             hy_b3=hy_b3, hy_w4=hy_w4, hy_freq=hy_freq, hy_decay=hy_decay, hy_bias=hy_bias)
    depth = ada_w.shape[0]
    weights = []
    for i in range(depth):
        wi1, wo1 = _chunk_ffn_weights(ffn1_wi[i], ffn1_wo[i])
        wi2, wo2 = _chunk_ffn_weights(ffn2_wi[i], ffn2_wo[i])
        weights.append(dict(wi1=wi1, wo1=wo1, wi2=wi2, wo2=wo2,
                            w_in=w_in[i].astype(BF16), w_out=w_out[i].astype(BF16)))
    nb = c_prompt.shape[0]
    mod = _modulation(jnp.concatenate([c_prompt, c_sample], axis=0), ada_w, ada_b)
    y_prompt = _trunk(x_prompt, mod[:, :nb], weights, p)
    y_sample = _trunk(x_sample, mod[:, nb:], weights, p)
    return (y_prompt, y_sample)
```

```python
import functools
import math

import jax
import jax.numpy as jnp
from jax import lax
from jax.experimental import pallas as pl
from jax.experimental.pallas import tpu as pltpu

F32 = jnp.float32
BF16 = jnp.bfloat16

D_MODEL = 1024
D_ATTN = 512
D_HY = 512
HEAD_DIM = 64
N_Q_HEADS = D_ATTN // HEAD_DIM
N_KV_HEADS = 2
KV_WIDTH = N_KV_HEADS * HEAD_DIM
ROT_DIM = HEAD_DIM // 4
ROPE_THETA = 500000.0
WINDOW = 128
FILTER_EMB = 33
FILTER_HID = 64
D_FF = 2816
N_IN = D_ATTN + 2 * KV_WIDTH + 3 * D_HY
N_MOD = 9
DEPTH = 4
ALPHA = float((2 * DEPTH) ** 0.25)
LN_EPS = 1e-5
RMS_EPS = 1e-6
NEG_BIG = -1e30
LOG2E = math.log2(math.e)
Q_SCALE = HEAD_DIM ** -0.5 * LOG2E

LANES = 128
VMEM_BYTES = 64 * 1024 * 1024
MIB = 1024 * 1024

ROW_TILE = 512
FFN_TILE = 1024
FFN_SUB = 512
FF_CHUNK = 256
Q_TILE = 512
Q_BLOCK = 128
BAND = 3 * Q_BLOCK
FFT_N2 = 256
FFT_LANES = 256
ROW_GROUP = 16
FFT_UNROLL = 8
MOD_N_TILE = 1536
HALO = 16


def _params(semantics, vmem_mib):
    return pltpu.CompilerParams(dimension_semantics=semantics,
                                vmem_limit_bytes=min(vmem_mib * MIB, VMEM_BYTES - 4 * MIB))


def _resident(shape):
    nd = len(shape)
    return pl.BlockSpec(shape, lambda *_: (0,) * nd, pipeline_mode=pl.Buffered(1))


def _dot(a, b):
    return jnp.dot(a, b, preferred_element_type=F32)


def _split(a):
    hi = a.astype(BF16)
    lo = (a - hi.astype(F32)).astype(BF16)
    return hi, lo


def _dot3(a, b):
    a_hi, a_lo = _split(a)
    b_hi, b_lo = _split(b)
    return _dot(a_hi, b_hi) + _dot(a_hi, b_lo) + _dot(a_lo, b_hi)


def _layer_norm(y, g, b):
    mu = jnp.mean(y, axis=-1, keepdims=True)
    d = y - mu
    var = jnp.mean(d * d, axis=-1, keepdims=True)
    return d * lax.rsqrt(var + LN_EPS) * g + b


def _rms_norm(y, g):
    ms = jnp.mean(y * y, axis=-1, keepdims=True)
    return y * lax.rsqrt(ms + RMS_EPS) * g


def _mod_kernel(c_ref, w_ref, b_ref, o_ref):
    c = c_ref[...]
    a = c * jax.nn.sigmoid(c)
    o_ref[0] = _dot3(a, w_ref[0]) + b_ref[0]


def _modulation(c, ada_w, ada_b):
    depth = ada_w.shape[0]
    b = c.shape[0]
    n = N_MOD * D_MODEL
    out = pl.pallas_call(
        _mod_kernel,
        out_shape=jax.ShapeDtypeStruct((depth, b, n), F32),
        grid=(depth, n // MOD_N_TILE),
        in_specs=[
            pl.BlockSpec((b, D_MODEL), lambda l, j: (0, 0)),
            pl.BlockSpec((1, D_MODEL, MOD_N_TILE), lambda l, j: (l, 0, j)),
            pl.BlockSpec((1, 1, MOD_N_TILE), lambda l, j: (l, 0, j)),
        ],
        out_specs=pl.BlockSpec((1, b, MOD_N_TILE), lambda l, j: (l, 0, j)),
        compiler_params=_params(("arbitrary", "arbitrary"), 40),
        name="adaln_mod",
    )(c, ada_w, ada_b.reshape(depth, 1, n))
    return out.reshape(depth, b, N_MOD, D_MODEL)


def _swiglu_rows(read_x, m, mod_row, wi_ref, wo_ref, g_ref, b_ref, h_scr, a_scr, s):
    shift, scale, gate = m[mod_row:mod_row + 1], m[mod_row + 1:mod_row + 2], m[mod_row + 2:mod_row + 3]
    h_scr[s] = (read_x() * (1.0 + scale) + shift).astype(BF16)
    for c in range(D_FF // FF_CHUNK):
        gu = _dot(h_scr[s], wi_ref[c])
        g, u = gu[:, :FF_CHUNK], gu[:, FF_CHUNK:]
        a_scr[s, :, c * FF_CHUNK:(c + 1) * FF_CHUNK] = (g * jax.nn.sigmoid(g) * u).astype(BF16)
    y = ALPHA * read_x() + (0.5 * (1.0 + gate)) * _dot(a_scr[s], wo_ref[...])
    return _layer_norm(y, g_ref[...], b_ref[...])


def _ffn_kernel(x_ref, mod_ref, wi_ref, wo_ref, g_ref, b_ref, o_ref, h_scr, a_scr):
    m = mod_ref[0]
    for s in range(h_scr.shape[0]):
        rows = slice(s * FFN_SUB, (s + 1) * FFN_SUB)
        o_ref[0, rows, :] = _swiglu_rows(lambda: x_ref[0, rows, :], m, 0, wi_ref, wo_ref, g_ref, b_ref,
                                         h_scr, a_scr, s)


def _mix_ffn_kernel(x_ref, oa_ref, oh_ref, mod_ref, w_out_ref, gn_ref, gm_ref, bm_ref, wi_ref, wo_ref, g_ref, b_ref,
                    o_ref, h_scr, a_scr, x_scr):
    m = mod_ref[0]
    gn = gn_ref[...]
    for s in range(h_scr.shape[0]):
        rows = slice(s * FFN_SUB, (s + 1) * FFN_SUB)
        a = _rms_norm(oa_ref[0, rows, :].astype(F32), gn[:, :D_ATTN]).astype(BF16)
        h = _rms_norm(oh_ref[0, rows, :].astype(F32), gn[:, D_ATTN:]).astype(BF16)
        o = _dot(a, w_out_ref[:D_ATTN, :]) + _dot(h, w_out_ref[D_ATTN:, :])
        x_scr[s] = _layer_norm(ALPHA * x_ref[0, rows, :] + (1.0 + m[5:6]) * o, gm_ref[...], bm_ref[...])
        o_ref[0, rows, :] = _swiglu_rows(lambda: x_scr[s], m, 6, wi_ref, wo_ref, g_ref, b_ref, h_scr, a_scr, s)


def _ffn(x, mod, wi_c, wo, ln_g, ln_b, mix=None):
    b, l, _ = x.shape
    tm = min(FFN_TILE, l)
    nsub = tm // FFN_SUB
    nck = D_FF // FF_CHUNK
    row = lambda i, j: (i, j, 0)
    vec = _resident((1, D_MODEL))
    x_spec = pl.BlockSpec((1, tm, D_MODEL), row)
    mod_spec = pl.BlockSpec((1, N_MOD, D_MODEL), lambda i, j: (i, 0, 0))
    ffn_specs = [_resident((nck, D_MODEL, 2 * FF_CHUNK)), _resident((D_FF, D_MODEL)), vec, vec]
    scratch = [pltpu.VMEM((nsub, FFN_SUB, D_MODEL), BF16), pltpu.VMEM((nsub, FFN_SUB, D_FF), BF16)]
    if mix is None:
        body, name = _ffn_kernel, "ffn"
        in_specs = [x_spec, mod_spec] + ffn_specs
        args = (x, mod, wi_c, wo, ln_g, ln_b)
    else:
        body, name = _mix_ffn_kernel, "mix_ffn"
        half = pl.BlockSpec((1, tm, D_ATTN), row)
        in_specs = [x_spec, half, half, mod_spec, _resident((D_MODEL, D_MODEL)), vec, vec, vec] + ffn_specs
        args = (x, mix[0], mix[1], mod, *mix[2:], wi_c, wo, ln_g, ln_b)
        scratch.append(pltpu.VMEM((nsub, FFN_SUB, D_MODEL), F32))
    return pl.pallas_call(
        body,
        out_shape=jax.ShapeDtypeStruct(x.shape, F32),
        grid=(b, l // tm),
        in_specs=in_specs,
        out_specs=x_spec,
        scratch_shapes=scratch,
        compiler_params=_params(("arbitrary", "arbitrary"), 60),
        name=name,
    )(*args)


def _rope_slab(t, cos, sin_up, sin_dn):
    return (t * cos + pltpu.roll(t, ROT_DIM // 2, 1) * sin_up
            + pltpu.roll(t, LANES - ROT_DIM // 2, 1) * sin_dn)


def _in_proj_kernel(x_ref, xp_ref, xn_ref, mod_ref, w_ref, cw_ref, cb_ref, cos_ref, sup_ref, sdn_ref,
                    q_ref, kd_ref, vd_ref, p_ref, x0_ref):
    j = pl.program_id(1)
    tm = x_ref.shape[1]
    m = mod_ref[0]
    x = jnp.concatenate([xp_ref[0], x_ref[0], xn_ref[0]], axis=0)
    h = (x * (1.0 + m[4:5]) + m[3:4]).astype(BF16)
    hz = _dot(h, w_ref[:, D_ATTN + 2 * KV_WIDTH:])
    head = jnp.where(j == 0, 0.0, hz[:HALO])
    tail = jnp.where(j == pl.num_programs(1) - 1, 0.0, hz[HALO + tm:])
    hz = jnp.concatenate([head, hz[HALO:HALO + tm], tail], axis=0)
    cw = cw_ref[...]
    inner = slice(HALO, HALO + tm)
    c = (pltpu.roll(hz, 1, 0)[inner] * cw[0:1] + hz[inner] * cw[1:2]
         + pltpu.roll(hz, tm + 2 * HALO - 1, 0)[inner] * cw[2:3] + cb_ref[...])
    p_ref[0] = (c[:, 2 * D_HY:] * c[:, D_HY:2 * D_HY]).astype(BF16)
    x0_ref[0] = c[:, :D_HY].astype(BF16)
    z = _dot(h[inner], w_ref[:, :D_ATTN + 2 * KV_WIDTH])
    cos, sup, sdn = cos_ref[...], sup_ref[...], sdn_ref[...]
    for s in range(D_ATTN // LANES):
        q = _rope_slab(z[:, s * LANES:(s + 1) * LANES], cos, sup, sdn)
        q_ref[0, :, s * LANES:(s + 1) * LANES] = (q * Q_SCALE).astype(BF16)
    k = _rope_slab(z[:, D_ATTN:D_ATTN + KV_WIDTH], cos, sup, sdn)
    v = z[:, D_ATTN + KV_WIDTH:D_ATTN + 2 * KV_WIDTH]
    lane = lax.broadcasted_iota(jnp.int32, k.shape, 1)
    first = lane < HEAD_DIM
    for t, ref in ((k, kd_ref), (v, vd_ref)):
        swapped = pltpu.roll(t, HEAD_DIM, 1)
        ref[0, :, :LANES] = jnp.where(first, t, swapped).astype(BF16)
        ref[0, :, LANES:] = jnp.where(first, swapped, t).astype(BF16)


def _in_proj(x, mod, w_in, conv_w, conv_b, rope):
    b, l, _ = x.shape
    tm = min(ROW_TILE, l)
    per = tm // HALO
    last = l // HALO - 1
    row = lambda i, j: (i, j, 0)
    tab = pl.BlockSpec((tm, LANES), lambda i, j: (j, 0))
    return pl.pallas_call(
        _in_proj_kernel,
        out_shape=(
            jax.ShapeDtypeStruct((b, l, D_ATTN), BF16),
            jax.ShapeDtypeStruct((b, l, 2 * KV_WIDTH), BF16),
            jax.ShapeDtypeStruct((b, l, 2 * KV_WIDTH), BF16),
            jax.ShapeDtypeStruct((b, l, D_HY), BF16),
            jax.ShapeDtypeStruct((b, l, D_HY), BF16),
        ),
        grid=(b, l // tm),
        in_specs=[
            pl.BlockSpec((1, tm, D_MODEL), row),
            pl.BlockSpec((1, HALO, D_MODEL), lambda i, j: (i, jnp.maximum(j * per - 1, 0), 0)),
            pl.BlockSpec((1, HALO, D_MODEL), lambda i, j: (i, jnp.minimum((j + 1) * per, last), 0)),
            pl.BlockSpec((1, N_MOD, D_MODEL), lambda i, j: (i, 0, 0)),
            _resident((D_MODEL, N_IN)),
            _resident((3, 3 * D_HY)),
            _resident((1, 3 * D_HY)),
            tab, tab, tab,
        ],
        out_specs=(
            pl.BlockSpec((1, tm, D_ATTN), row),
            pl.BlockSpec((1, tm, 2 * KV_WIDTH), row),
            pl.BlockSpec((1, tm, 2 * KV_WIDTH), row),
            pl.BlockSpec((1, tm, D_HY), row),
            pl.BlockSpec((1, tm, D_HY), row),
        ),
        compiler_params=_params(("arbitrary", "arbitrary"), 48),
        name="in_proj",
    )(x, x, x, mod, w_in, conv_w, conv_b.reshape(1, -1), *rope)


def _rope_tables(l):
    half = ROT_DIM // 2
    inv = ROPE_THETA ** (-jnp.arange(0, ROT_DIM, 2, dtype=F32) / ROT_DIM)
    ang = jnp.arange(l, dtype=F32)[:, None] * inv[None]
    cos, sin = jnp.cos(ang), jnp.sin(ang)
    rest = HEAD_DIM - ROT_DIM
    cos_h = jnp.concatenate([cos, cos, jnp.ones((l, rest), F32)], axis=1)
    up_h = jnp.concatenate([jnp.zeros((l, half), F32), sin, jnp.zeros((l, rest), F32)], axis=1)
    dn_h = jnp.concatenate([-sin, jnp.zeros((l, half + rest), F32)], axis=1)
    reps = LANES // HEAD_DIM
    return tuple(jnp.tile(t, (1, reps)) for t in (cos_h, up_h, dn_h))


def _attn_kernel(sink_ref, q_ref, kd_ref, vd_ref, o_ref, *, seq):
    j = pl.program_id(1)
    first = lax.broadcasted_iota(jnp.int32, (BAND, LANES), 1) < HEAD_DIM
    out_first = lax.broadcasted_iota(jnp.int32, (Q_BLOCK, LANES), 1) < HEAD_DIM
    zero = jnp.zeros((BAND, LANES), BF16)
    ones_ext = jnp.concatenate([jnp.where(first, 1.0, 0.0), jnp.where(first, 0.0, 1.0)], axis=0).astype(BF16)
    for blk in range(q_ref.shape[1] // Q_BLOCK):
        q0 = j * q_ref.shape[1] + blk * Q_BLOCK
        start = pl.multiple_of(jnp.clip(q0 - Q_BLOCK, 0, seq - BAND), Q_BLOCK)
        qpos = q0 + lax.broadcasted_iota(jnp.int32, (Q_BLOCK, BAND), 0)
        kpos = start + lax.broadcasted_iota(jnp.int32, (Q_BLOCK, BAND), 1)
        bias = jnp.where(jnp.abs(qpos - kpos) <= WINDOW, 0.0, NEG_BIG).astype(F32)
        rows = slice(blk * Q_BLOCK, (blk + 1) * Q_BLOCK)
        for kh in range(N_KV_HEADS):
            cols = slice(kh * LANES, (kh + 1) * LANES)
            kslab = kd_ref[0, pl.ds(start, BAND), cols]
            vslab = vd_ref[0, pl.ds(start, BAND), cols]
            kb = jnp.concatenate([jnp.where(first, kslab, zero), jnp.where(first, zero, kslab)], axis=0)
            vb = jnp.concatenate([jnp.where(first, vslab, zero), jnp.where(first, zero, vslab)], axis=0)
            vb_ext = jnp.concatenate([vb, ones_ext], axis=1)
            q = jnp.concatenate([q_ref[0, rows, (2 * kh + r) * LANES:(2 * kh + r + 1) * LANES] for r in range(2)],
                                axis=0)
            s = lax.dot_general(q, kb, (((1,), (1,)), ((), ())), preferred_element_type=F32)
            probs, sink_terms = [], []
            for r in range(2):
                pr, sk = [], []
                for e in range(2):
                    sink = sink_ref[4 * kh + 2 * r + e] * LOG2E
                    se = s[r * Q_BLOCK:(r + 1) * Q_BLOCK, e * BAND:(e + 1) * BAND] + bias
                    mx = jnp.maximum(jnp.max(se, axis=1, keepdims=True), sink)
                    pr.append(jnp.exp2(se - mx).astype(BF16))
                    sk.append(jnp.exp2(sink - mx))
                probs.append(jnp.concatenate(pr, axis=1))
                sink_terms.append(jnp.where(out_first, sk[0], sk[1]))
            o = _dot(jnp.concatenate(probs, axis=0), vb_ext)
            for r in range(2):
                part = o[r * Q_BLOCK:(r + 1) * Q_BLOCK]
                den = part[:, LANES:] + sink_terms[r]
                o_ref[0, rows, (2 * kh + r) * LANES:(2 * kh + r + 1) * LANES] = (
                    part[:, :LANES] / den).astype(o_ref.dtype)


def _attention(q, kd, vd, sink):
    b, l, _ = q.shape
    tq = min(Q_TILE, l)
    return pl.pallas_call(
        functools.partial(_attn_kernel, seq=l),
        out_shape=jax.ShapeDtypeStruct((b, l, D_ATTN), BF16),
        grid=(b, l // tq),
        in_specs=[
            pl.BlockSpec(memory_space=pltpu.SMEM),
            pl.BlockSpec((1, tq, D_ATTN), lambda i, j: (i, j, 0)),
            pl.BlockSpec((1, l, 2 * KV_WIDTH), lambda i, j: (i, 0, 0)),
            pl.BlockSpec((1, l, 2 * KV_WIDTH), lambda i, j: (i, 0, 0)),
        ],
        out_specs=pl.BlockSpec((1, tq, D_ATTN), lambda i, j: (i, j, 0)),
        compiler_params=_params(("arbitrary", "arbitrary"), 32),
        name="window_attn",
    )(sink, q, kd, vd)


def _filter_kernel(z_ref, w1_ref, b1_ref, w2_ref, b2_ref, w3_ref, b3_ref, w4_ref, fr_ref, dec_ref,
                   e_ref, d_ref):
    z = z_ref[...]
    fr = fr_ref[...]
    h = jnp.sin(fr * (_dot3(z, w1_ref[...]) + b1_ref[...]))
    h = jnp.sin(fr * (_dot3(h, w2_ref[...]) + b2_ref[...]))
    h = jnp.sin(fr * (_dot3(h, w3_ref[...]) + b3_ref[...]))
    filt = _dot3(h, w4_ref[...]) * jnp.exp(-z[:, 0:1] * jnp.abs(dec_ref[...]))
    fwd, bwd = filt[:, :D_HY], filt[:, D_HY:]
    e_ref[...] = (fwd + bwd).astype(BF16)
    d_ref[...] = (fwd - bwd).astype(BF16)


def _filter_features(l):
    t = jnp.linspace(0.0, 1.0, l, dtype=F32)[:, None]
    n_bands = (FILTER_EMB - 1) // 2
    w = 2.0 * math.pi * jnp.arange(l, dtype=F32)[:, None] / l
    fb = jnp.linspace(1e-4, n_bands - 1, n_bands, dtype=F32)[None]
    z = jnp.concatenate([t, jnp.cos(fb * w), -jnp.sin(fb * w)], axis=-1)
    return jnp.pad(z, ((0, 0), (0, FILTER_HID - FILTER_EMB)))


def _filters(z, w1, b1, w2, b2, w3, b3, w4, freq, decay):
    l = z.shape[0]
    tl = min(ROW_TILE, l)
    w1p = jnp.pad(w1, ((0, FILTER_HID - FILTER_EMB), (0, 0)))
    vec = lambda a: a.reshape(1, -1)
    out = jax.ShapeDtypeStruct((l, D_HY), BF16)
    tile = pl.BlockSpec((tl, D_HY), lambda i: (i, 0))
    return pl.pallas_call(
        _filter_kernel,
        out_shape=(out, out),
        grid=(l // tl,),
        in_specs=[
            pl.BlockSpec((tl, FILTER_HID), lambda i: (i, 0)),
            _resident((FILTER_HID, FILTER_HID)), _resident((1, FILTER_HID)),
            _resident((FILTER_HID, FILTER_HID)), _resident((1, FILTER_HID)),
            _resident((FILTER_HID, FILTER_HID)), _resident((1, FILTER_HID)),
            _resident((FILTER_HID, 2 * D_HY)), _resident((1, FILTER_HID)), _resident((1, 2 * D_HY)),
        ],
        out_specs=(tile, tile),
        compiler_params=_params(("arbitrary",), 32),
        name="hyena_filter",
    )(z, w1p, vec(b1), w2, vec(b2), w3, vec(b3), w4, vec(freq), vec(decay))


def _fft_matrices(l):
    n = 2 * l
    n2 = FFT_N2
    n1 = n // n2
    k1 = jnp.arange(n1, dtype=jnp.int32)
    t1 = jnp.arange(n1 // 2, dtype=jnp.int32)
    ang1 = ((k1[:, None] * t1[None, :]) % n1).astype(F32) * (2.0 * math.pi / n1)
    f1 = jnp.stack([jnp.cos(ang1), jnp.sin(ang1)], axis=1).reshape(2 * n1, n1 // 2)
    eye = jnp.eye(ROW_GROUP, dtype=F32)
    kron = (f1[:, None, :, None] * eye[None, :, None, :]).reshape(2 * n1 * ROW_GROUP, (n1 // 2) * ROW_GROUP)
    k2 = jnp.arange(n2 // 2, dtype=jnp.int32)
    t2 = jnp.arange(n2, dtype=jnp.int32)
    k = k1[:, None] + n1 * k2[None, :]
    ang2 = ((k[:, :, None] * t2[None, None, :]) % n).astype(F32) * (2.0 * math.pi / n)
    c, s = jnp.cos(ang2), jnp.sin(ang2)
    re_rows = jnp.concatenate([c, -s], axis=2)
    im_rows = jnp.concatenate([s, c], axis=2)
    nyq = jnp.concatenate([jnp.where(t2 % 2 == 0, 1.0, -1.0).astype(F32), jnp.zeros((n2,), F32)])
    im_rows = im_rows.at[0, 0, :].set(nyq)
    g = jnp.concatenate([re_rows, im_rows], axis=1)
    return dict(kron=kron.astype(BF16), kron_t=kron.T.astype(BF16),
                g=g.astype(BF16), g_t=g.transpose(0, 2, 1).astype(BF16))


def _fft_stage1(x_ref, kron_ref, b_scr):
    n1h, n2, c = x_ref.shape

    def group(g, carry):
        grp = _row_group(g)
        x = x_ref[:, grp, :].reshape(n1h * ROW_GROUP, c)
        b_scr[:, grp, :] = _dot(kron_ref[...], x).astype(BF16).reshape(4 * n1h, ROW_GROUP, c)
        return carry

    lax.fori_loop(0, n2 // ROW_GROUP, group, 0, unroll=FFT_UNROLL)


def _row_group(g):
    return pl.ds(pl.multiple_of(g * ROW_GROUP, ROW_GROUP), ROW_GROUP)


def _fft_stage2(g_ref, b_scr, k):
    n2, c = b_scr.shape[1:]
    return _dot(g_ref[k], b_scr[pl.ds(2 * k, 2)].reshape(2 * n2, c))


def _spectrum_kernel(e_ref, d_ref, kron_ref, g_ref, ka_ref, kb_ref, ka2_ref, be_scr, bd_scr, *, seq):
    half = ka_ref.shape[1]
    _fft_stage1(e_ref, kron_ref, be_scr)
    _fft_stage1(d_ref, kron_ref, bd_scr)
    first_row = lax.broadcasted_iota(jnp.int32, (half, 1), 0) == 0

    def freq(k, carry):
        ue = _fft_stage2(g_ref, be_scr, k)
        ud = _fft_stage2(g_ref, bd_scr, k)
        is_dc = first_row & (k == 0)
        ka = ue[:half] * jnp.where(is_dc, 0.5 / seq, 1.0 / seq)
        ka_ref[k] = ka
        kb_ref[k] = jnp.where(is_dc, 0.0, ud[half:] * (1.0 / seq))
        ka2_ref[k] = jnp.where(is_dc, ue[half:half + 1] * (0.5 / seq), ka)
        return carry

    lax.fori_loop(0, ka_ref.shape[0], freq, 0)


def _spectrum(mats, e, d):
    l = e.shape[0]
    n1, n2 = mats["g"].shape[:2]
    cw = LANES
    out = jax.ShapeDtypeStruct((n1, n2 // 2, D_HY), F32)
    src = pl.BlockSpec((n1 // 2, n2, cw), lambda h: (0, 0, h))
    dst = pl.BlockSpec((n1, n2 // 2, cw), lambda h: (0, 0, h))
    view = lambda a: a.reshape(n1 // 2, n2, D_HY)
    return pl.pallas_call(
        functools.partial(_spectrum_kernel, seq=l),
        out_shape=(out, out, out),
        grid=(D_HY // cw,),
        in_specs=[src, src, _resident(mats["kron"].shape), _resident(mats["g"].shape)],
        out_specs=(dst, dst, dst),
        scratch_shapes=[pltpu.VMEM((2 * n1, n2, cw), BF16), pltpu.VMEM((2 * n1, n2, cw), BF16)],
        compiler_params=_params(("arbitrary",), 56),
        name="hyena_spectrum",
    )(view(e), view(d), mats["kron"], mats["g"])


def _long_conv_kernel(p_ref, x0_ref, kron_ref, kron_t_ref, g_ref, gt_ref, ka_ref, kb_ref, ka2_ref, bias_ref,
                      o_ref, b_scr):
    n1h, n2, c = p_ref.shape[1:]
    half = n2 // 2
    _fft_stage1(p_ref.at[0], kron_ref, b_scr)

    def freq(k, carry):
        u = _fft_stage2(g_ref, b_scr, k)
        ure, uim = u[:half], u[half:]
        ka, kb, ka2 = ka_ref[k], kb_ref[k], ka2_ref[k]
        y = jnp.concatenate([ure * ka - uim * kb, ure * kb + uim * ka2], axis=0).astype(BF16)
        b_scr[pl.ds(2 * k, 2)] = _dot(gt_ref[k], y).astype(BF16).reshape(2, n2, c)
        return carry

    lax.fori_loop(0, 2 * n1h, freq, 0, unroll=FFT_UNROLL)
    bias = bias_ref[...]

    def group(g, carry):
        grp = _row_group(g)
        z = b_scr[:, grp, :].reshape(4 * n1h * ROW_GROUP, c)
        y = _dot(kron_t_ref[...], z).reshape(n1h, ROW_GROUP, c)
        gated = (y + p_ref[0, :, grp, :].astype(F32) * bias) * x0_ref[0, :, grp, :].astype(F32)
        o_ref[0, :, grp, :] = gated.astype(BF16)
        return carry

    lax.fori_loop(0, n2 // ROW_GROUP, group, 0, unroll=FFT_UNROLL)


def _long_conv(p, x0, mats, ka, kb, ka2, bias):
    b, l, _ = p.shape
    n1, n2 = mats["g"].shape[:2]
    cw = FFT_LANES
    view = lambda a: a.reshape(b, n1 // 2, n2, D_HY)
    tile = pl.BlockSpec((1, n1 // 2, n2, cw), lambda h, i: (i, 0, 0, h))
    spec = pl.BlockSpec((n1, n2 // 2, cw), lambda h, i: (0, 0, h), pipeline_mode=pl.Buffered(1))
    out = pl.pallas_call(
        _long_conv_kernel,
        out_shape=jax.ShapeDtypeStruct((b, n1 // 2, n2, D_HY), BF16),
        grid=(D_HY // cw, b),
        in_specs=[
            tile, tile,
            _resident(mats["kron"].shape), _resident(mats["kron_t"].shape),
            _resident(mats["g"].shape), _resident(mats["g_t"].shape),
            spec, spec, spec,
            pl.BlockSpec((1, cw), lambda h, i: (0, h)),
        ],
        out_specs=tile,
        scratch_shapes=[pltpu.VMEM((2 * n1, n2, cw), BF16)],
        compiler_params=_params(("arbitrary", "arbitrary"), 60),
        name="hyena_long_conv",
    )(view(p), view(x0), mats["kron"], mats["kron_t"], mats["g"], mats["g_t"], ka, kb, ka2, bias.reshape(1, -1))
    return out.reshape(b, l, D_HY)


def _chunk_ffn_weights(wi, wo):
    nck = D_FF // FF_CHUNK
    g = wi[:, :D_FF].reshape(D_MODEL, nck, FF_CHUNK)
    u = wi[:, D_FF:].reshape(D_MODEL, nck, FF_CHUNK)
    wi_c = jnp.concatenate([g, u], axis=2).transpose(1, 0, 2).astype(BF16)
    return wi_c, wo.astype(BF16)


def _trunk(x, mod, weights, p):
    l = x.shape[1]
    depth = mod.shape[0]
    rope = _rope_tables(l)
    feats = _filter_features(l)
    mats = _fft_matrices(l)
    vec = lambda a: a.reshape(1, -1)
    for i in range(depth):
        w = weights[i]
        m = mod[i]
        x = _ffn(x, m, w["wi1"], w["wo1"], vec(p["ln_g"][i, 0]), vec(p["ln_b"][i, 0]))
        q, kd, vd, gated, x0 = _in_proj(x, m, w["w_in"], p["hy_conv_w"][i], p["hy_conv_b"][i], rope)
        o_attn = _attention(q, kd, vd, p["sink"][i])
        e, d = _filters(feats, p["hy_w1"][i], p["hy_b1"][i], p["hy_w2"][i], p["hy_b2"][i],
                        p["hy_w3"][i], p["hy_b3"][i], p["hy_w4"][i], p["hy_freq"][i], p["hy_decay"][i])
        ka, kb, ka2 = _spectrum(mats, e, d)
        o_hy = _long_conv(gated, x0, mats, ka, kb, ka2, p["hy_bias"][i])
        mix = (o_attn, o_hy, w["w_out"], vec(p["grp_norm_g"][i]), vec(p["ln_g"][i, 1]), vec(p["ln_b"][i, 1]))
        x = _ffn(x, m, w["wi2"], w["wo2"], vec(p["ln_g"][i, 2]), vec(p["ln_b"][i, 2]), mix)
    return x


def kernel(x_prompt, x_sample, c_prompt, c_sample, ada_w, ada_b, ffn1_wi, ffn1_wo, ffn2_wi, ffn2_wo, ln_g, ln_b, w_in, w_out, sink, grp_norm_g, hy_conv_w, hy_conv_b, hy_w1, hy_b1, hy_w2, hy_b2, hy_w3, hy_b3, hy_w4, hy_freq, hy_decay, hy_bias):
    p = dict(ln_g=ln_g, ln_b=ln_b, sink=sink, grp_norm_g=grp_norm_g, hy_conv_w=hy_conv_w,
             hy_conv_b=hy_conv_b, hy_w1=hy_w1, hy_b1=hy_b1, hy_w2=hy_w2, hy_b2=hy_b2, hy_w3=hy_w3,
             hy_b3=hy_b3, hy_w4=hy_w4, hy_freq=hy_freq, hy_decay=hy_decay, hy_bias=hy_bias)
    depth = ada_w.shape[0]
    weights = []
    for i in range(depth):
        wi1, wo1 = _chunk_ffn_weights(ffn1_wi[i], ffn1_wo[i])
        wi2, wo2 = _chunk_ffn_weights(ffn2_wi[i], ffn2_wo[i])
        weights.append(dict(wi1=wi1, wo1=wo1, wi2=wi2, wo2=wo2,
                            w_in=w_in[i].astype(BF16), w_out=w_out[i].astype(BF16)))
    nb = c_prompt.shape[0]
    mod = _modulation(jnp.concatenate([c_prompt, c_sample], axis=0), ada_w, ada_b)
    y_prompt = _trunk(x_prompt, mod[:, :nb], weights, p)
    y_sample = _trunk(x_sample, mod[:, nb:], weights, p)
    return (y_prompt, y_sample)
```

```python
import functools
import math

import jax
import jax.numpy as jnp
from jax import lax
from jax.experimental import pallas as pl
from jax.experimental.pallas import tpu as pltpu

F32 = jnp.float32
BF16 = jnp.bfloat16

D_MODEL = 1024
D_ATTN = 512
D_HY = 512
HEAD_DIM = 64
N_Q_HEADS = D_ATTN // HEAD_DIM
N_KV_HEADS = 2
KV_WIDTH = N_KV_HEADS * HEAD_DIM
ROT_DIM = HEAD_DIM // 4
ROPE_THETA = 500000.0
WINDOW = 128
FILTER_EMB = 33
FILTER_HID = 64
D_FF = 2816
N_IN = D_ATTN + 2 * KV_WIDTH + 3 * D_HY
N_MOD = 9
DEPTH = 4
ALPHA = float((2 * DEPTH) ** 0.25)
LN_EPS = 1e-5
RMS_EPS = 1e-6
NEG_BIG = -1e30
LOG2E = math.log2(math.e)
Q_SCALE = HEAD_DIM ** -0.5 * LOG2E

LANES = 128
KV_SLABS = 2 * N_KV_HEADS * LANES
VMEM_BYTES = 64 * 1024 * 1024
MIB = 1024 * 1024

ROW_TILE = 512
FFN_TILE = 1024
FFN_SUB = 512
FF_CHUNK = 256
LN_ROWS = 256
Q_TILE = 512
Q_BLOCK = 128
BAND = 3 * Q_BLOCK
FFT_N2 = 256
FFT_LANES = 256
ROW_GROUP = 16
FFT_UNROLL = 8
MOD_N_TILE = 1536
HALO = 16
CONV_LANES = 256


def _params(semantics, vmem_mib):
    return pltpu.CompilerParams(dimension_semantics=semantics,
                                vmem_limit_bytes=min(vmem_mib * MIB, VMEM_BYTES - 4 * MIB))


def _resident(shape):
    nd = len(shape)
    return pl.BlockSpec(shape, lambda *_: (0,) * nd, pipeline_mode=pl.Buffered(1))


def _dot(a, b):
    return jnp.dot(a, b, preferred_element_type=F32)


def _split(a):
    hi = a.astype(BF16)
    lo = (a - hi.astype(F32)).astype(BF16)
    return hi, lo


def _dot3(a, b):
    a_hi, a_lo = _split(a)
    b_hi, b_lo = _split(b)
    return _dot(a_hi, b_hi) + _dot(a_hi, b_lo) + _dot(a_lo, b_hi)


def _layer_norm(y, g, b):
    mu = jnp.mean(y, axis=-1, keepdims=True)
    d = y - mu
    var = jnp.mean(d * d, axis=-1, keepdims=True)
    return d * lax.rsqrt(var + LN_EPS) * g + b


def _rms_norm(y, g):
    ms = jnp.mean(y * y, axis=-1, keepdims=True)
    return y * lax.rsqrt(ms + RMS_EPS) * g


def _mod_kernel(c_ref, w_ref, b_ref, o_ref):
    c = c_ref[...]
    a = c * jax.nn.sigmoid(c)
    o_ref[0] = _dot3(a, w_ref[0]) + b_ref[0]


def _modulation(c, ada_w, ada_b):
    depth = ada_w.shape[0]
    b = c.shape[0]
    n = N_MOD * D_MODEL
    out = pl.pallas_call(
        _mod_kernel,
        out_shape=jax.ShapeDtypeStruct((depth, b, n), F32),
        grid=(depth, n // MOD_N_TILE),
        in_specs=[
            pl.BlockSpec((b, D_MODEL), lambda l, j: (0, 0)),
            pl.BlockSpec((1, D_MODEL, MOD_N_TILE), lambda l, j: (l, 0, j)),
            pl.BlockSpec((1, 1, MOD_N_TILE), lambda l, j: (l, 0, j)),
        ],
        out_specs=pl.BlockSpec((1, b, MOD_N_TILE), lambda l, j: (l, 0, j)),
        compiler_params=_params(("arbitrary", "arbitrary"), 40),
        name="adaln_mod",
    )(c, ada_w, ada_b.reshape(depth, 1, n))
    return out.reshape(depth, b, N_MOD, D_MODEL)


def _swiglu_rows(x_rows, o_rows, m, mod_row, wi_ref, wo_ref, g_ref, b_ref, h_scr, a_scr, s):
    shift, scale, gate = m[mod_row:mod_row + 1], m[mod_row + 1:mod_row + 2], m[mod_row + 2:mod_row + 3]
    for piece in _ln_pieces():
        h_scr[s, piece, :] = (x_rows[piece, :] * (1.0 + scale) + shift).astype(BF16)
    for c in range(D_FF // FF_CHUNK):
        gu = _dot(h_scr[s], wi_ref[c])
        g, u = gu[:, :FF_CHUNK], gu[:, FF_CHUNK:]
        a_scr[s, :, c * FF_CHUNK:(c + 1) * FF_CHUNK] = (g * jax.nn.sigmoid(g) * u).astype(BF16)
    half_gate = 0.5 * (1.0 + gate)
    for piece in _ln_pieces():
        y = ALPHA * x_rows[piece, :] + half_gate * _dot(a_scr[s, piece, :], wo_ref[...])
        o_rows[piece, :] = _layer_norm(y, g_ref[...], b_ref[...])


def _ln_pieces():
    return [slice(r * LN_ROWS, (r + 1) * LN_ROWS) for r in range(FFN_SUB // LN_ROWS)]


def _ffn_kernel(x_ref, mod_ref, wi_ref, wo_ref, g_ref, b_ref, o_ref, h_scr, a_scr):
    m = mod_ref[0]
    for s in range(h_scr.shape[0]):
        rows = pl.ds(s * FFN_SUB, FFN_SUB)
        _swiglu_rows(x_ref.at[0, rows], o_ref.at[0, rows], m, 0, wi_ref, wo_ref, g_ref, b_ref, h_scr, a_scr, s)


def _mix_ffn_kernel(x_ref, oa_ref, oh_ref, mod_ref, w_out_ref, gn_ref, gm_ref, bm_ref, wi_ref, wo_ref, g_ref, b_ref,
                    o_ref, h_scr, a_scr, x_scr):
    m = mod_ref[0]
    gn = gn_ref[...]
    for s in range(h_scr.shape[0]):
        for piece in _ln_pieces():
            rows = slice(s * FFN_SUB + piece.start, s * FFN_SUB + piece.stop)
            a = _rms_norm(oa_ref[0, rows, :].astype(F32), gn[:, :D_ATTN]).astype(BF16)
            h = _rms_norm(oh_ref[0, rows, :].astype(F32), gn[:, D_ATTN:]).astype(BF16)
            o = _dot(a, w_out_ref[:D_ATTN, :]) + _dot(h, w_out_ref[D_ATTN:, :])
            x_scr[s, piece, :] = _layer_norm(ALPHA * x_ref[0, rows, :] + (1.0 + m[5:6]) * o,
                                             gm_ref[...], bm_ref[...])
        _swiglu_rows(x_scr.at[s], o_ref.at[0, pl.ds(s * FFN_SUB, FFN_SUB)], m, 6, wi_ref, wo_ref, g_ref, b_ref,
                     h_scr, a_scr, s)


def _ffn(x, mod, wi_c, wo, ln_g, ln_b, mix=None):
    b, l, _ = x.shape
    tm = min(FFN_TILE, l)
    nsub = tm // FFN_SUB
    nck = D_FF // FF_CHUNK
    row = lambda i, j: (i, j, 0)
    vec = _resident((1, D_MODEL))
    x_spec = pl.BlockSpec((1, tm, D_MODEL), row)
    mod_spec = pl.BlockSpec((1, N_MOD, D_MODEL), lambda i, j: (i, 0, 0))
    ffn_specs = [_resident((nck, D_MODEL, 2 * FF_CHUNK)), _resident((D_FF, D_MODEL)), vec, vec]
    scratch = [pltpu.VMEM((nsub, FFN_SUB, D_MODEL), BF16), pltpu.VMEM((nsub, FFN_SUB, D_FF), BF16)]
    if mix is None:
        body, name = _ffn_kernel, "ffn"
        in_specs = [x_spec, mod_spec] + ffn_specs
        args = (x, mod, wi_c, wo, ln_g, ln_b)
    else:
        body, name = _mix_ffn_kernel, "mix_ffn"
        half = pl.BlockSpec((1, tm, D_ATTN), row)
        in_specs = [x_spec, half, half, mod_spec, _resident((D_MODEL, D_MODEL)), vec, vec, vec] + ffn_specs
        args = (x, mix[0], mix[1], mod, *mix[2:], wi_c, wo, ln_g, ln_b)
        scratch.append(pltpu.VMEM((nsub, FFN_SUB, D_MODEL), F32))
    return pl.pallas_call(
        body,
        out_shape=jax.ShapeDtypeStruct(x.shape, F32),
        grid=(b, l // tm),
        in_specs=in_specs,
        out_specs=x_spec,
        scratch_shapes=scratch,
        compiler_params=_params(("arbitrary", "arbitrary"), 60),
        name=name,
    )(*args)


def _rope_slab(t, cos, sin_up, sin_dn):
    return (t * cos + pltpu.roll(t, ROT_DIM // 2, 1) * sin_up
            + pltpu.roll(t, LANES - ROT_DIM // 2, 1) * sin_dn)


def _in_proj_kernel(x_ref, xp_ref, xn_ref, mod_ref, w_ref, cw_ref, cb_ref, cos_ref, sup_ref, sdn_ref,
                    q_ref, kd_ref, vd_ref, p_ref, x0_ref):
    j = pl.program_id(1)
    tm = x_ref.shape[1]
    m = mod_ref[0]
    x = jnp.concatenate([xp_ref[0], x_ref[0], xn_ref[0]], axis=0)
    h = (x * (1.0 + m[4:5]) + m[3:4]).astype(BF16)
    inner = slice(HALO, HALO + tm)
    first_tile, last_tile = j == 0, j == pl.num_programs(1) - 1

    def conv(col):
        lanes = slice(col, col + CONV_LANES)
        z = _dot(h, w_ref[:, D_ATTN + 2 * KV_WIDTH + col:D_ATTN + 2 * KV_WIDTH + col + CONV_LANES])
        z = jnp.concatenate([jnp.where(first_tile, 0.0, z[:HALO]), z[inner],
                             jnp.where(last_tile, 0.0, z[HALO + tm:])], axis=0)
        return (pltpu.roll(z, 1, 0)[inner] * cw_ref[0:1, lanes] + z[inner] * cw_ref[1:2, lanes]
                + pltpu.roll(z, tm + 2 * HALO - 1, 0)[inner] * cw_ref[2:3, lanes] + cb_ref[:, lanes])

    z = _dot(h[inner], w_ref[:, :D_ATTN + 2 * KV_WIDTH])
    cos, sup, sdn = cos_ref[...], sup_ref[...], sdn_ref[...]
    for s in range(D_ATTN // LANES):
        q = _rope_slab(z[:, s * LANES:(s + 1) * LANES], cos, sup, sdn)
        q_ref[0, :, s * LANES:(s + 1) * LANES] = (q * Q_SCALE).astype(BF16)
    k = _rope_slab(z[:, D_ATTN:D_ATTN + KV_WIDTH], cos, sup, sdn)
    v = z[:, D_ATTN + KV_WIDTH:D_ATTN + 2 * KV_WIDTH]
    lane = lax.broadcasted_iota(jnp.int32, k.shape, 1)
    first = lane < HEAD_DIM
    for t, ref in ((k, kd_ref), (v, vd_ref)):
        swapped = pltpu.roll(t, HEAD_DIM, 1)
        slabs = (jnp.where(first, t, 0.0), jnp.where(first, 0.0, swapped),
                 jnp.where(first, swapped, 0.0), jnp.where(first, 0.0, t))
        for i, slab in enumerate(slabs):
            ref[0, :, i * LANES:(i + 1) * LANES] = slab.astype(BF16)
    for t in range(D_HY // CONV_LANES):
        lanes = slice(t * CONV_LANES, (t + 1) * CONV_LANES)
        x0, x1, hv = (conv(g * D_HY + t * CONV_LANES) for g in range(3))
        p_ref[0, :, lanes] = (hv * x1).astype(BF16)
        x0_ref[0, :, lanes] = x0.astype(BF16)


def _in_proj(x, mod, w_in, conv_w, conv_b, rope):
    b, l, _ = x.shape
    tm = min(ROW_TILE, l)
    per = tm // HALO
    last = l // HALO - 1
    row = lambda i, j: (i, j, 0)
    tab = pl.BlockSpec((tm, LANES), lambda i, j: (j, 0))
    return pl.pallas_call(
        _in_proj_kernel,
        out_shape=(
            jax.ShapeDtypeStruct((b, l, D_ATTN), BF16),
            jax.ShapeDtypeStruct((b, l, KV_SLABS), BF16),
            jax.ShapeDtypeStruct((b, l, KV_SLABS), BF16),
            jax.ShapeDtypeStruct((b, l, D_HY), BF16),
            jax.ShapeDtypeStruct((b, l, D_HY), BF16),
        ),
        grid=(b, l // tm),
        in_specs=[
            pl.BlockSpec((1, tm, D_MODEL), row),
            pl.BlockSpec((1, HALO, D_MODEL), lambda i, j: (i, jnp.maximum(j * per - 1, 0), 0)),
            pl.BlockSpec((1, HALO, D_MODEL), lambda i, j: (i, jnp.minimum((j + 1) * per, last), 0)),
            pl.BlockSpec((1, N_MOD, D_MODEL), lambda i, j: (i, 0, 0)),
            _resident((D_MODEL, N_IN)),
            _resident((3, 3 * D_HY)),
            _resident((1, 3 * D_HY)),
            tab, tab, tab,
        ],
        out_specs=(
            pl.BlockSpec((1, tm, D_ATTN), row),
            pl.BlockSpec((1, tm, KV_SLABS), row),
            pl.BlockSpec((1, tm, KV_SLABS), row),
            pl.BlockSpec((1, tm, D_HY), row),
            pl.BlockSpec((1, tm, D_HY), row),
        ),
        compiler_params=_params(("arbitrary", "arbitrary"), 48),
        name="in_proj",
    )(x, x, x, mod, w_in, conv_w, conv_b.reshape(1, -1), *rope)


def _rope_tables(l):
    half = ROT_DIM // 2
    inv = ROPE_THETA ** (-jnp.arange(0, ROT_DIM, 2, dtype=F32) / ROT_DIM)
    ang = jnp.arange(l, dtype=F32)[:, None] * inv[None]
    cos, sin = jnp.cos(ang), jnp.sin(ang)
    rest = HEAD_DIM - ROT_DIM
    cos_h = jnp.concatenate([cos, cos, jnp.ones((l, rest), F32)], axis=1)
    up_h = jnp.concatenate([jnp.zeros((l, half), F32), sin, jnp.zeros((l, rest), F32)], axis=1)
    dn_h = jnp.concatenate([-sin, jnp.zeros((l, half + rest), F32)], axis=1)
    reps = LANES // HEAD_DIM
    return tuple(jnp.tile(t, (1, reps)) for t in (cos_h, up_h, dn_h))


def _attn_kernel(sink_ref, q_ref, kd_ref, vd_ref, o_ref, *, seq):
    j = pl.program_id(1)
    first = lax.broadcasted_iota(jnp.int32, (BAND, LANES), 1) < HEAD_DIM
    out_first = lax.broadcasted_iota(jnp.int32, (Q_BLOCK, LANES), 1) < HEAD_DIM
    ones_ext = jnp.concatenate([jnp.where(first, 1.0, 0.0), jnp.where(first, 0.0, 1.0)], axis=0).astype(BF16)
    for blk in range(q_ref.shape[1] // Q_BLOCK):
        q0 = j * q_ref.shape[1] + blk * Q_BLOCK
        start = pl.multiple_of(jnp.clip(q0 - Q_BLOCK, 0, seq - BAND), Q_BLOCK)
        qpos = q0 + lax.broadcasted_iota(jnp.int32, (Q_BLOCK, BAND), 0)
        kpos = start + lax.broadcasted_iota(jnp.int32, (Q_BLOCK, BAND), 1)
        bias = jnp.where(jnp.abs(qpos - kpos) <= WINDOW, 0.0, NEG_BIG).astype(F32)
        rows = slice(blk * Q_BLOCK, (blk + 1) * Q_BLOCK)
        for kh in range(N_KV_HEADS):
            kb, vb = (jnp.concatenate([ref[0, pl.ds(start, BAND), (2 * kh + e) * LANES:(2 * kh + e + 1) * LANES]
                                       for e in range(2)], axis=0) for ref in (kd_ref, vd_ref))
            vb_ext = jnp.concatenate([vb, ones_ext], axis=1)
            q = jnp.concatenate([q_ref[0, rows, (2 * kh + r) * LANES:(2 * kh + r + 1) * LANES] for r in range(2)],
                                axis=0)
            s = lax.dot_general(q, kb, (((1,), (1,)), ((), ())), preferred_element_type=F32)
            probs, sink_terms = [], []
            for r in range(2):
                pr, sk = [], []
                for e in range(2):
                    sink = sink_ref[4 * kh + 2 * r + e] * LOG2E
                    se = s[r * Q_BLOCK:(r + 1) * Q_BLOCK, e * BAND:(e + 1) * BAND] + bias
                    mx = jnp.maximum(jnp.max(se, axis=1, keepdims=True), sink)
                    pr.append(jnp.exp2(se - mx).astype(BF16))
                    sk.append(jnp.exp2(sink - mx))
                probs.append(jnp.concatenate(pr, axis=1))
                sink_terms.append(jnp.where(out_first, sk[0], sk[1]))
            o = _dot(jnp.concatenate(probs, axis=0), vb_ext)
            for r in range(2):
                part = o[r * Q_BLOCK:(r + 1) * Q_BLOCK]
                den = part[:, LANES:] + sink_terms[r]
                o_ref[0, rows, (2 * kh + r) * LANES:(2 * kh + r + 1) * LANES] = (
                    part[:, :LANES] / den).astype(o_ref.dtype)


def _attention(q, kd, vd, sink):
    b, l, _ = q.shape
    tq = min(Q_TILE, l)
    return pl.pallas_call(
        functools.partial(_attn_kernel, seq=l),
        out_shape=jax.ShapeDtypeStruct((b, l, D_ATTN), BF16),
        grid=(b, l // tq),
        in_specs=[
            pl.BlockSpec(memory_space=pltpu.SMEM),
            pl.BlockSpec((1, tq, D_ATTN), lambda i, j: (i, j, 0)),
            pl.BlockSpec((1, l, KV_SLABS), lambda i, j: (i, 0, 0)),
            pl.BlockSpec((1, l, KV_SLABS), lambda i, j: (i, 0, 0)),
        ],
        out_specs=pl.BlockSpec((1, tq, D_ATTN), lambda i, j: (i, j, 0)),
        compiler_params=_params(("arbitrary", "arbitrary"), 32),
        name="window_attn",
    )(sink, q, kd, vd)


def _filter_kernel(z_ref, w1_ref, b1_ref, w2_ref, b2_ref, w3_ref, b3_ref, w4_ref, fr_ref, dec_ref,
                   e_ref, d_ref):
    z = z_ref[...]
    fr = fr_ref[...]
    h = jnp.sin(fr * (_dot3(z, w1_ref[...]) + b1_ref[...]))
    h = jnp.sin(fr * (_dot3(h, w2_ref[...]) + b2_ref[...]))
    h = jnp.sin(fr * (_dot3(h, w3_ref[...]) + b3_ref[...]))
    filt = _dot3(h, w4_ref[...]) * jnp.exp(-z[:, 0:1] * jnp.abs(dec_ref[...]))
    fwd, bwd = filt[:, :D_HY], filt[:, D_HY:]
    e_ref[...] = (fwd + bwd).astype(BF16)
    d_ref[...] = (fwd - bwd).astype(BF16)


def _filter_features(l):
    t = jnp.linspace(0.0, 1.0, l, dtype=F32)[:, None]
    n_bands = (FILTER_EMB - 1) // 2
    w = 2.0 * math.pi * jnp.arange(l, dtype=F32)[:, None] / l
    fb = jnp.linspace(1e-4, n_bands - 1, n_bands, dtype=F32)[None]
    z = jnp.concatenate([t, jnp.cos(fb * w), -jnp.sin(fb * w)], axis=-1)
    return jnp.pad(z, ((0, 0), (0, FILTER_HID - FILTER_EMB)))


def _filters(z, w1, b1, w2, b2, w3, b3, w4, freq, decay):
    l = z.shape[0]
    tl = min(ROW_TILE, l)
    w1p = jnp.pad(w1, ((0, FILTER_HID - FILTER_EMB), (0, 0)))
    vec = lambda a: a.reshape(1, -1)
    out = jax.ShapeDtypeStruct((l, D_HY), BF16)
    tile = pl.BlockSpec((tl, D_HY), lambda i: (i, 0))
    return pl.pallas_call(
        _filter_kernel,
        out_shape=(out, out),
        grid=(l // tl,),
        in_specs=[
            pl.BlockSpec((tl, FILTER_HID), lambda i: (i, 0)),
            _resident((FILTER_HID, FILTER_HID)), _resident((1, FILTER_HID)),
            _resident((FILTER_HID, FILTER_HID)), _resident((1, FILTER_HID)),
            _resident((FILTER_HID, FILTER_HID)), _resident((1, FILTER_HID)),
            _resident((FILTER_HID, 2 * D_HY)), _resident((1, FILTER_HID)), _resident((1, 2 * D_HY)),
        ],
        out_specs=(tile, tile),
        compiler_params=_params(("arbitrary",), 32),
        name="hyena_filter",
    )(z, w1p, vec(b1), w2, vec(b2), w3, vec(b3), w4, vec(freq), vec(decay))


def _fft_matrices(l):
    n = 2 * l
    n2 = FFT_N2
    n1 = n // n2
    k1 = jnp.arange(n1, dtype=jnp.int32)
    t1 = jnp.arange(n1 // 2, dtype=jnp.int32)
    ang1 = ((k1[:, None] * t1[None, :]) % n1).astype(F32) * (2.0 * math.pi / n1)
    f1 = jnp.stack([jnp.cos(ang1), jnp.sin(ang1)], axis=1).reshape(2 * n1, n1 // 2)
    eye = jnp.eye(ROW_GROUP, dtype=F32)
    kron = (f1[:, None, :, None] * eye[None, :, None, :]).reshape(2 * n1 * ROW_GROUP, (n1 // 2) * ROW_GROUP)
    k2 = jnp.arange(n2 // 2, dtype=jnp.int32)
    t2 = jnp.arange(n2, dtype=jnp.int32)
    k = k1[:, None] + n1 * k2[None, :]
    ang2 = ((k[:, :, None] * t2[None, None, :]) % n).astype(F32) * (2.0 * math.pi / n)
    c, s = jnp.cos(ang2), jnp.sin(ang2)
    re_rows = jnp.concatenate([c, -s], axis=2)
    im_rows = jnp.concatenate([s, c], axis=2)
    nyq = jnp.concatenate([jnp.where(t2 % 2 == 0, 1.0, -1.0).astype(F32), jnp.zeros((n2,), F32)])
    im_rows = im_rows.at[0, 0, :].set(nyq)
    g = jnp.concatenate([re_rows, im_rows], axis=1)
    return dict(kron=kron.astype(BF16), kron_t=kron.T.astype(BF16),
                g=g.astype(BF16), g_t=g.transpose(0, 2, 1).astype(BF16))


def _fft_stage1(x_ref, kron_ref, b_scr):
    n1h, n2, c = x_ref.shape

    def group(g, carry):
        grp = _row_group(g)
        x = x_ref[:, grp, :].reshape(n1h * ROW_GROUP, c)
        b_scr[:, grp, :] = _dot(kron_ref[...], x).astype(BF16).reshape(4 * n1h, ROW_GROUP, c)
        return carry

    lax.fori_loop(0, n2 // ROW_GROUP, group, 0, unroll=FFT_UNROLL)


def _row_group(g):
    return pl.ds(pl.multiple_of(g * ROW_GROUP, ROW_GROUP), ROW_GROUP)


def _fft_stage2(g_ref, b_scr, k):
    n2, c = b_scr.shape[1:]
    return _dot(g_ref[k], b_scr[pl.ds(2 * k, 2)].reshape(2 * n2, c))


def _spectrum_kernel(e_ref, d_ref, kron_ref, g_ref, ka_ref, kb_ref, ka2_ref, be_scr, bd_scr, *, seq):
    half = ka_ref.shape[1]
    _fft_stage1(e_ref, kron_ref, be_scr)
    _fft_stage1(d_ref, kron_ref, bd_scr)
    first_row = lax.broadcasted_iota(jnp.int32, (half, 1), 0) == 0

    def freq(k, carry):
        n2, c = be_scr.shape[1:]
        be = be_scr[pl.ds(2 * k, 2)].reshape(2 * n2, c)
        bd = bd_scr[pl.ds(2 * k, 2)].reshape(2 * n2, c)
        re = _dot(g_ref[k, :half, :], be)
        im = _dot(g_ref[k, half:, :], bd)
        nyq = _dot(g_ref[k, half:half + ROW_GROUP, :], be)[0:1]
        is_dc = first_row & (k == 0)
        ka = re * jnp.where(is_dc, 0.5 / seq, 1.0 / seq)
        ka_ref[k] = ka
        kb_ref[k] = jnp.where(is_dc, 0.0, im * (1.0 / seq))
        ka2_ref[k] = jnp.where(is_dc, nyq * (0.5 / seq), ka)
        return carry

    lax.fori_loop(0, ka_ref.shape[0], freq, 0, unroll=FFT_UNROLL)


def _spectrum(mats, e, d):
    l = e.shape[0]
    n1, n2 = mats["g"].shape[:2]
    cw = LANES
    out = jax.ShapeDtypeStruct((n1, n2 // 2, D_HY), F32)
    src = pl.BlockSpec((n1 // 2, n2, cw), lambda h: (0, 0, h))
    dst = pl.BlockSpec((n1, n2 // 2, cw), lambda h: (0, 0, h))
    view = lambda a: a.reshape(n1 // 2, n2, D_HY)
    return pl.pallas_call(
        functools.partial(_spectrum_kernel, seq=l),
        out_shape=(out, out, out),
        grid=(D_HY // cw,),
        in_specs=[src, src, _resident(mats["kron"].shape), _resident(mats["g"].shape)],
        out_specs=(dst, dst, dst),
        scratch_shapes=[pltpu.VMEM((2 * n1, n2, cw), BF16), pltpu.VMEM((2 * n1, n2, cw), BF16)],
        compiler_params=_params(("arbitrary",), 56),
        name="hyena_spectrum",
    )(view(e), view(d), mats["kron"], mats["g"])


def _long_conv_kernel(p_ref, x0_ref, kron_ref, kron_t_ref, g_ref, gt_ref, ka_ref, kb_ref, ka2_ref, bias_ref,
                      o_ref, b_scr):
    n1h, n2, c = p_ref.shape[1:]
    half = n2 // 2
    _fft_stage1(p_ref.at[0], kron_ref, b_scr)

    def freq(k, carry):
        u = _fft_stage2(g_ref, b_scr, k)
        ure, uim = u[:half], u[half:]
        ka, kb, ka2 = ka_ref[k], kb_ref[k], ka2_ref[k]
        y = jnp.concatenate([ure * ka - uim * kb, ure * kb + uim * ka2], axis=0).astype(BF16)
        b_scr[pl.ds(2 * k, 2)] = _dot(gt_ref[k], y).astype(BF16).reshape(2, n2, c)
        return carry

    lax.fori_loop(0, 2 * n1h, freq, 0, unroll=FFT_UNROLL)
    bias = bias_ref[...]

    def group(g, carry):
        grp = _row_group(g)
        z = b_scr[:, grp, :].reshape(4 * n1h * ROW_GROUP, c)
        y = _dot(kron_t_ref[...], z).reshape(n1h, ROW_GROUP, c)
        gated = (y + p_ref[0, :, grp, :].astype(F32) * bias) * x0_ref[0, :, grp, :].astype(F32)
        o_ref[0, :, grp, :] = gated.astype(BF16)
        return carry

    lax.fori_loop(0, n2 // ROW_GROUP, group, 0, unroll=FFT_UNROLL)


def _long_conv(p, x0, mats, ka, kb, ka2, bias):
    b, l, _ = p.shape
    n1, n2 = mats["g"].shape[:2]
    cw = FFT_LANES
    view = lambda a: a.reshape(b, n1 // 2, n2, D_HY)
    tile = pl.BlockSpec((1, n1 // 2, n2, cw), lambda h, i: (i, 0, 0, h))
    spec = pl.BlockSpec((n1, n2 // 2, cw), lambda h, i: (0, 0, h), pipeline_mode=pl.Buffered(1))
    out = pl.pallas_call(
        _long_conv_kernel,
        out_shape=jax.ShapeDtypeStruct((b, n1 // 2, n2, D_HY), BF16),
        grid=(D_HY // cw, b),
        in_specs=[
            tile, tile,
            _resident(mats["kron"].shape), _resident(mats["kron_t"].shape),
            _resident(mats["g"].shape), _resident(mats["g_t"].shape),
            spec, spec, spec,
            pl.BlockSpec((1, cw), lambda h, i: (0, h)),
        ],
        out_specs=tile,
        scratch_shapes=[pltpu.VMEM((2 * n1, n2, cw), BF16)],
        compiler_params=_params(("arbitrary", "arbitrary"), 60),
        name="hyena_long_conv",
    )(view(p), view(x0), mats["kron"], mats["kron_t"], mats["g"], mats["g_t"], ka, kb, ka2, bias.reshape(1, -1))
    return out.reshape(b, l, D_HY)


def _chunk_ffn_weights(wi, wo):
    nck = D_FF // FF_CHUNK
    g = wi[:, :D_FF].reshape(D_MODEL, nck, FF_CHUNK)
    u = wi[:, D_FF:].reshape(D_MODEL, nck, FF_CHUNK)
    wi_c = jnp.concatenate([g, u], axis=2).transpose(1, 0, 2).astype(BF16)
    return wi_c, wo.astype(BF16)


def _trunk(x, mod, weights, p):
    l = x.shape[1]
    depth = mod.shape[0]
    rope = _rope_tables(l)
    feats = _filter_features(l)
    mats = _fft_matrices(l)
    vec = lambda a: a.reshape(1, -1)
    for i in range(depth):
        w = weights[i]
        m = mod[i]
        x = _ffn(x, m, w["wi1"], w["wo1"], vec(p["ln_g"][i, 0]), vec(p["ln_b"][i, 0]))
        q, kd, vd, gated, x0 = _in_proj(x, m, w["w_in"], p["hy_conv_w"][i], p["hy_conv_b"][i], rope)
        o_attn = _attention(q, kd, vd, p["sink"][i])
        e, d = _filters(feats, p["hy_w1"][i], p["hy_b1"][i], p["hy_w2"][i], p["hy_b2"][i],
                        p["hy_w3"][i], p["hy_b3"][i], p["hy_w4"][i], p["hy_freq"][i], p["hy_decay"][i])
        ka, kb, ka2 = _spectrum(mats, e, d)
        o_hy = _long_conv(gated, x0, mats, ka, kb, ka2, p["hy_bias"][i])
        mix = (o_attn, o_hy, w["w_out"], vec(p["grp_norm_g"][i]), vec(p["ln_g"][i, 1]), vec(p["ln_b"][i, 1]))
        x = _ffn(x, m, w["wi2"], w["wo2"], vec(p["ln_g"][i, 2]), vec(p["ln_b"][i, 2]), mix)
    return x


def kernel(x_prompt, x_sample, c_prompt, c_sample, ada_w, ada_b, ffn1_wi, ffn1_wo, ffn2_wi, ffn2_wo, ln_g, ln_b, w_in, w_out, sink, grp_norm_g, hy_conv_w, hy_conv_b, hy_w1, hy_b1, hy_w2, hy_b2, hy_w3, hy_b3, hy_w4, hy_freq, hy_decay, hy_bias):
    p = dict(ln_g=ln_g, ln_b=ln_b, sink=sink, grp_norm_g=grp_norm_g, hy_conv_w=hy_conv_w,
             hy_conv_b=hy_conv_b, hy_w1=hy_w1, hy_b1=hy_b1, hy_w2=hy_w2, hy_b2=hy_b2, hy_w3=hy_w3,
             hy_b3=hy_b3, hy_w4=hy_w4, hy_freq=hy_freq, hy_decay=hy_decay, hy_bias=hy_bias)
    depth = ada_w.shape[0]
    weights = []
    for i in range(depth):
        wi1, wo1 = _chunk_ffn_weights(ffn1_wi[i], ffn1_wo[i])
        wi2, wo2 = _chunk_ffn_weights(ffn2_wi[i], ffn2_wo[i])
        weights.append(dict(wi1=wi1, wo1=wo1, wi2=wi2, wo2=wo2,
                            w_in=w_in[i].astype(BF16), w_out=w_out[i].astype(BF16)))
    nb = c_prompt.shape[0]
    mod = _modulation(jnp.concatenate([c_prompt, c_sample], axis=0), ada_w, ada_b)
    y_prompt = _trunk(x_prompt, mod[:, :nb], weights, p)
    y_sample = _trunk(x_sample, mod[:, nb:], weights, p)
    return (y_prompt, y_sample)
```

```python
import functools
import math

import jax
import jax.numpy as jnp
from jax import lax
from jax.experimental import pallas as pl
from jax.experimental.pallas import tpu as pltpu

F32 = jnp.float32
BF16 = jnp.bfloat16

D_MODEL = 1024
D_ATTN = 512
D_HY = 512
HEAD_DIM = 64
N_Q_HEADS = D_ATTN // HEAD_DIM
N_KV_HEADS = 2
KV_WIDTH = N_KV_HEADS * HEAD_DIM
ROT_DIM = HEAD_DIM // 4
ROPE_THETA = 500000.0
WINDOW = 128
FILTER_EMB = 33
FILTER_HID = 64
D_FF = 2816
N_IN = D_ATTN + 2 * KV_WIDTH + 3 * D_HY
N_MOD = 9
DEPTH = 4
ALPHA = float((2 * DEPTH) ** 0.25)
LN_EPS = 1e-5
RMS_EPS = 1e-6
NEG_BIG = -1e30
LOG2E = math.log2(math.e)
Q_SCALE = HEAD_DIM ** -0.5 * LOG2E

LANES = 128
KV_SLABS = 2 * N_KV_HEADS * LANES
VMEM_BYTES = 64 * 1024 * 1024
MIB = 1024 * 1024

ROW_TILE = 1024
FFN_TILE = 1024
FFN_SUB = 512
FF_CHUNK = 256
LN_ROWS = 256
Q_TILE = 1024
Q_BLOCK = 128
BAND = 3 * Q_BLOCK
FFT_N2 = 256
FFT_LANES = 256
ROW_GROUP = 16
FFT_UNROLL = 8
MOD_N_TILE = 1536
HALO = 16
CONV_LANES = 256


def _params(semantics, vmem_mib):
    return pltpu.CompilerParams(dimension_semantics=semantics,
                                vmem_limit_bytes=min(vmem_mib * MIB, VMEM_BYTES - 4 * MIB))


def _resident(shape):
    nd = len(shape)
    return pl.BlockSpec(shape, lambda *_: (0,) * nd, pipeline_mode=pl.Buffered(1))


def _dot(a, b):
    return jnp.dot(a, b, preferred_element_type=F32)


def _split(a):
    hi = a.astype(BF16)
    lo = (a - hi.astype(F32)).astype(BF16)
    return hi, lo


def _dot3(a, b):
    a_hi, a_lo = _split(a)
    b_hi, b_lo = _split(b)
    return _dot(a_hi, b_hi) + _dot(a_hi, b_lo) + _dot(a_lo, b_hi)


def _layer_norm(y, g, b):
    mu = jnp.mean(y, axis=-1, keepdims=True)
    d = y - mu
    var = jnp.mean(d * d, axis=-1, keepdims=True)
    return d * lax.rsqrt(var + LN_EPS) * g + b


def _rms_norm(y, g):
    ms = jnp.mean(y * y, axis=-1, keepdims=True)
    return y * lax.rsqrt(ms + RMS_EPS) * g


def _mod_kernel(c_ref, w_ref, b_ref, o_ref):
    c = c_ref[...]
    a = c * jax.nn.sigmoid(c)
    o_ref[0] = _dot3(a, w_ref[0]) + b_ref[0]


def _modulation(c, ada_w, ada_b):
    depth = ada_w.shape[0]
    b = c.shape[0]
    n = N_MOD * D_MODEL
    out = pl.pallas_call(
        _mod_kernel,
        out_shape=jax.ShapeDtypeStruct((depth, b, n), F32),
        grid=(depth, n // MOD_N_TILE),
        in_specs=[
            pl.BlockSpec((b, D_MODEL), lambda l, j: (0, 0)),
            pl.BlockSpec((1, D_MODEL, MOD_N_TILE), lambda l, j: (l, 0, j)),
            pl.BlockSpec((1, 1, MOD_N_TILE), lambda l, j: (l, 0, j)),
        ],
        out_specs=pl.BlockSpec((1, b, MOD_N_TILE), lambda l, j: (l, 0, j)),
        compiler_params=_params(("arbitrary", "arbitrary"), 40),
        name="adaln_mod",
    )(c, ada_w, ada_b.reshape(depth, 1, n))
    return out.reshape(depth, b, N_MOD, D_MODEL)


def _ln_pieces():
    return [slice(r * LN_ROWS, (r + 1) * LN_ROWS) for r in range(FFN_SUB // LN_ROWS)]


def _modulate_rows(x, m, mod_row):
    return (x * (1.0 + m[mod_row + 1:mod_row + 2]) + m[mod_row:mod_row + 1]).astype(BF16)


def _swiglu_rows(x_rows, h_rows, a_rows, o_rows, gate, wi_ref, wo_ref, g_ref, b_ref):
    for c in range(D_FF // FF_CHUNK):
        gu = _dot(h_rows[...], wi_ref[c])
        g, u = gu[:, :FF_CHUNK], gu[:, FF_CHUNK:]
        a_rows[:, c * FF_CHUNK:(c + 1) * FF_CHUNK] = (g * jax.nn.sigmoid(g) * u).astype(BF16)
    half_gate = 0.5 * (1.0 + gate)
    for piece in _ln_pieces():
        y = ALPHA * x_rows[piece, :] + half_gate * _dot(a_rows[piece, :], wo_ref[...])
        o_rows[piece, :] = _layer_norm(y, g_ref[...], b_ref[...])


def _ffn_kernel(x_ref, mod_ref, wi_ref, wo_ref, g_ref, b_ref, o_ref, h_scr, a_scr):
    m = mod_ref[0]
    for s in range(h_scr.shape[0]):
        rows = pl.ds(s * FFN_SUB, FFN_SUB)
        for piece in _ln_pieces():
            h_scr[s, piece, :] = _modulate_rows(x_ref[0, pl.ds(s * FFN_SUB + piece.start, LN_ROWS), :], m, 0)
        _swiglu_rows(x_ref.at[0, rows], h_scr.at[s], a_scr.at[s], o_ref.at[0, rows], m[2:3],
                     wi_ref, wo_ref, g_ref, b_ref)


def _mix_ffn_kernel(x_hi, oa_hi, oh_hi, x_lo, oa_lo, oh_lo, mod_ref, mod_next_ref, w_out_ref, gn_ref, gm_ref, bm_ref,
                    wi_ref, wo_ref, g_ref, b_ref, o_ref, h_scr, a_scr, x_scr, x_pre, h_pre):
    i = pl.program_id(0)
    gn = gn_ref[...]

    def mix_rows(x_rows, oa_rows, oh_rows, m, x_dst, h_dst):
        for piece in _ln_pieces():
            a = _rms_norm(oa_rows[piece, :].astype(F32), gn[:, :D_ATTN]).astype(BF16)
            h = _rms_norm(oh_rows[piece, :].astype(F32), gn[:, D_ATTN:]).astype(BF16)
            o = _dot(a, w_out_ref[:D_ATTN, :]) + _dot(h, w_out_ref[D_ATTN:, :])
            x = _layer_norm(ALPHA * x_rows[piece, :] + (1.0 + m[5:6]) * o, gm_ref[...], bm_ref[...])
            x_dst[piece, :] = x
            h_dst[piece, :] = _modulate_rows(x, m, 6)

    @pl.when(i == 0)
    def _():
        mix_rows(x_lo.at[0], oa_lo.at[0], oh_lo.at[0], mod_next_ref[0], x_pre.at[0], h_pre.at[0])

    @pl.when(i > 0)
    def _():
        m = mod_ref[0]
        cur = (i - 1) % 2
        ffn = functools.partial(_swiglu_rows, gate=m[8:9], wi_ref=wi_ref, wo_ref=wo_ref, g_ref=g_ref, b_ref=b_ref)
        mix_rows(x_hi.at[0], oa_hi.at[0], oh_hi.at[0], m, x_scr, h_scr)
        ffn(x_pre.at[cur], h_pre.at[cur], a_scr.at[0], o_ref.at[0, pl.ds(0, FFN_SUB)])
        ffn(x_scr, h_scr, a_scr.at[1], o_ref.at[0, pl.ds(FFN_SUB, FFN_SUB)])
        mix_rows(x_lo.at[0], oa_lo.at[0], oh_lo.at[0], mod_next_ref[0], x_pre.at[1 - cur], h_pre.at[1 - cur])


def _ffn(x, mod, wi_c, wo, ln_g, ln_b):
    b, l, _ = x.shape
    tm = min(FFN_TILE, l)
    nsub = tm // FFN_SUB
    nck = D_FF // FF_CHUNK
    vec = _resident((1, D_MODEL))
    x_spec = pl.BlockSpec((1, tm, D_MODEL), lambda i, j: (i, j, 0))
    return pl.pallas_call(
        _ffn_kernel,
        out_shape=jax.ShapeDtypeStruct(x.shape, F32),
        grid=(b, l // tm),
        in_specs=[x_spec, pl.BlockSpec((1, N_MOD, D_MODEL), lambda i, j: (i, 0, 0)),
                  _resident((nck, D_MODEL, 2 * FF_CHUNK)), _resident((D_FF, D_MODEL)), vec, vec],
        out_specs=x_spec,
        scratch_shapes=[pltpu.VMEM((nsub, FFN_SUB, D_MODEL), BF16), pltpu.VMEM((nsub, FFN_SUB, D_FF), BF16)],
        compiler_params=_params(("arbitrary", "arbitrary"), 60),
        name="ffn",
    )(x, mod, wi_c, wo, ln_g, ln_b)


def _mix_ffn(x, o_attn, o_hy, mod, w_out, grp_g, mix_g, mix_b, wi_c, wo, ln_g, ln_b):
    b, l, _ = x.shape
    assert l % FFN_TILE == 0 and FFN_TILE == 2 * FFN_SUB
    nt = l // FFN_TILE
    n = b * nt
    nck = D_FF // FF_CHUNK
    vec = _resident((1, D_MODEL))
    cur = lambda i: jnp.maximum(i - 1, 0)
    nxt = lambda i: jnp.minimum(i, n - 1)
    hi = lambda i: (cur(i) // nt, 2 * (cur(i) % nt) + 1, 0)
    lo = lambda i: (nxt(i) // nt, 2 * (nxt(i) % nt), 0)
    rows = lambda width, index: pl.BlockSpec((1, FFN_SUB, width), index)
    return pl.pallas_call(
        _mix_ffn_kernel,
        out_shape=jax.ShapeDtypeStruct(x.shape, F32),
        grid=(n + 1,),
        in_specs=[
            rows(D_MODEL, hi), rows(D_ATTN, hi), rows(D_HY, hi),
            rows(D_MODEL, lo), rows(D_ATTN, lo), rows(D_HY, lo),
            pl.BlockSpec((1, N_MOD, D_MODEL), lambda i: (cur(i) // nt, 0, 0)),
            pl.BlockSpec((1, N_MOD, D_MODEL), lambda i: (nxt(i) // nt, 0, 0)),
            _resident((D_MODEL, D_MODEL)), vec, vec, vec,
            _resident((nck, D_MODEL, 2 * FF_CHUNK)), _resident((D_FF, D_MODEL)), vec, vec,
        ],
        out_specs=pl.BlockSpec((1, FFN_TILE, D_MODEL), lambda i: (cur(i) // nt, cur(i) % nt, 0)),
        scratch_shapes=[
            pltpu.VMEM((FFN_SUB, D_MODEL), BF16), pltpu.VMEM((2, FFN_SUB, D_FF), BF16),
            pltpu.VMEM((FFN_SUB, D_MODEL), F32),
            pltpu.VMEM((2, FFN_SUB, D_MODEL), F32), pltpu.VMEM((2, FFN_SUB, D_MODEL), BF16),
        ],
        compiler_params=_params(("arbitrary",), 60),
        name="mix_ffn",
    )(x, o_attn, o_hy, x, o_attn, o_hy, mod, mod, w_out, grp_g, mix_g, mix_b, wi_c, wo, ln_g, ln_b)


def _rope_slab(t, cos, sin_up, sin_dn):
    return (t * cos + pltpu.roll(t, ROT_DIM // 2, 1) * sin_up
            + pltpu.roll(t, LANES - ROT_DIM // 2, 1) * sin_dn)


def _in_proj_kernel(x_ref, xp_ref, xn_ref, mod_ref, w_ref, cw_ref, cb_ref, cos_ref, sup_ref, sdn_ref,
                    q_ref, kd_ref, vd_ref, p_ref, x0_ref):
    j = pl.program_id(1)
    tm = x_ref.shape[1]
    m = mod_ref[0]
    x = jnp.concatenate([xp_ref[0], x_ref[0], xn_ref[0]], axis=0)
    h = (x * (1.0 + m[4:5]) + m[3:4]).astype(BF16)
    inner = slice(HALO, HALO + tm)
    first_tile, last_tile = j == 0, j == pl.num_programs(1) - 1

    def conv(col):
        lanes = slice(col, col + CONV_LANES)
        z = _dot(h, w_ref[:, D_ATTN + 2 * KV_WIDTH + col:D_ATTN + 2 * KV_WIDTH + col + CONV_LANES])
        z = jnp.concatenate([jnp.where(first_tile, 0.0, z[:HALO]), z[inner],
                             jnp.where(last_tile, 0.0, z[HALO + tm:])], axis=0)
        return (pltpu.roll(z, 1, 0)[inner] * cw_ref[0:1, lanes] + z[inner] * cw_ref[1:2, lanes]
                + pltpu.roll(z, tm + 2 * HALO - 1, 0)[inner] * cw_ref[2:3, lanes] + cb_ref[:, lanes])

    z = _dot(h[inner], w_ref[:, :D_ATTN + 2 * KV_WIDTH])
    cos, sup, sdn = cos_ref[...], sup_ref[...], sdn_ref[...]
    for s in range(D_ATTN // LANES):
        q = _rope_slab(z[:, s * LANES:(s + 1) * LANES], cos, sup, sdn)
        q_ref[0, :, s * LANES:(s + 1) * LANES] = (q * Q_SCALE).astype(BF16)
    k = _rope_slab(z[:, D_ATTN:D_ATTN + KV_WIDTH], cos, sup, sdn)
    v = z[:, D_ATTN + KV_WIDTH:D_ATTN + 2 * KV_WIDTH]
    lane = lax.broadcasted_iota(jnp.int32, k.shape, 1)
    first = lane < HEAD_DIM
    for t, ref in ((k, kd_ref), (v, vd_ref)):
        swapped = pltpu.roll(t, HEAD_DIM, 1)
        slabs = (jnp.where(first, t, 0.0), jnp.where(first, 0.0, swapped),
                 jnp.where(first, swapped, 0.0), jnp.where(first, 0.0, t))
        for i, slab in enumerate(slabs):
            ref[0, :, i * LANES:(i + 1) * LANES] = slab.astype(BF16)
    for t in range(D_HY // CONV_LANES):
        lanes = slice(t * CONV_LANES, (t + 1) * CONV_LANES)
        x0, x1, hv = (conv(g * D_HY + t * CONV_LANES) for g in range(3))
        p_ref[0, :, lanes] = (hv * x1).astype(BF16)
        x0_ref[0, :, lanes] = x0.astype(BF16)


def _in_proj(x, mod, w_in, conv_w, conv_b, rope):
    b, l, _ = x.shape
    tm = min(ROW_TILE, l)
    per = tm // HALO
    last = l // HALO - 1
    row = lambda i, j: (i, j, 0)
    tab = pl.BlockSpec((tm, LANES), lambda i, j: (j, 0))
    return pl.pallas_call(
        _in_proj_kernel,
        out_shape=(
            jax.ShapeDtypeStruct((b, l, D_ATTN), BF16),
            jax.ShapeDtypeStruct((b, l, KV_SLABS), BF16),
            jax.ShapeDtypeStruct((b, l, KV_SLABS), BF16),
            jax.ShapeDtypeStruct((b, l, D_HY), BF16),
            jax.ShapeDtypeStruct((b, l, D_HY), BF16),
        ),
        grid=(b, l // tm),
        in_specs=[
            pl.BlockSpec((1, tm, D_MODEL), row),
            pl.BlockSpec((1, HALO, D_MODEL), lambda i, j: (i, jnp.maximum(j * per - 1, 0), 0)),
            pl.BlockSpec((1, HALO, D_MODEL), lambda i, j: (i, jnp.minimum((j + 1) * per, last), 0)),
            pl.BlockSpec((1, N_MOD, D_MODEL), lambda i, j: (i, 0, 0)),
            _resident((D_MODEL, N_IN)),
            _resident((3, 3 * D_HY)),
            _resident((1, 3 * D_HY)),
            tab, tab, tab,
        ],
        out_specs=(
            pl.BlockSpec((1, tm, D_ATTN), row),
            pl.BlockSpec((1, tm, KV_SLABS), row),
            pl.BlockSpec((1, tm, KV_SLABS), row),
            pl.BlockSpec((1, tm, D_HY), row),
            pl.BlockSpec((1, tm, D_HY), row),
        ),
        compiler_params=_params(("arbitrary", "arbitrary"), 48),
        name="in_proj",
    )(x, x, x, mod, w_in, conv_w, conv_b.reshape(1, -1), *rope)


def _rope_tables(l):
    half = ROT_DIM // 2
    inv = ROPE_THETA ** (-jnp.arange(0, ROT_DIM, 2, dtype=F32) / ROT_DIM)
    ang = jnp.arange(l, dtype=F32)[:, None] * inv[None]
    cos, sin = jnp.cos(ang), jnp.sin(ang)
    rest = HEAD_DIM - ROT_DIM
    cos_h = jnp.concatenate([cos, cos, jnp.ones((l, rest), F32)], axis=1)
    up_h = jnp.concatenate([jnp.zeros((l, half), F32), sin, jnp.zeros((l, rest), F32)], axis=1)
    dn_h = jnp.concatenate([-sin, jnp.zeros((l, half + rest), F32)], axis=1)
    reps = LANES // HEAD_DIM
    return tuple(jnp.tile(t, (1, reps)) for t in (cos_h, up_h, dn_h))


def _attn_kernel(sink_ref, q_ref, kd_ref, vd_ref, o_ref, *, seq):
    j = pl.program_id(1)
    first = lax.broadcasted_iota(jnp.int32, (BAND, LANES), 1) < HEAD_DIM
    out_first = lax.broadcasted_iota(jnp.int32, (Q_BLOCK, LANES), 1) < HEAD_DIM
    ones_ext = jnp.concatenate([jnp.where(first, 1.0, 0.0), jnp.where(first, 0.0, 1.0)], axis=0).astype(BF16)
    for blk in range(q_ref.shape[1] // Q_BLOCK):
        q0 = j * q_ref.shape[1] + blk * Q_BLOCK
        start = pl.multiple_of(jnp.clip(q0 - Q_BLOCK, 0, seq - BAND), Q_BLOCK)
        qpos = q0 + lax.broadcasted_iota(jnp.int32, (Q_BLOCK, BAND), 0)
        kpos = start + lax.broadcasted_iota(jnp.int32, (Q_BLOCK, BAND), 1)
        bias = jnp.where(jnp.abs(qpos - kpos) <= WINDOW, 0.0, NEG_BIG).astype(F32)
        rows = slice(blk * Q_BLOCK, (blk + 1) * Q_BLOCK)
        for kh in range(N_KV_HEADS):
            kb, vb = (jnp.concatenate([ref[0, pl.ds(start, BAND), (2 * kh + e) * LANES:(2 * kh + e + 1) * LANES]
                                       for e in range(2)], axis=0) for ref in (kd_ref, vd_ref))
            vb_ext = jnp.concatenate([vb, ones_ext], axis=1)
            q = jnp.concatenate([q_ref[0, rows, (2 * kh + r) * LANES:(2 * kh + r + 1) * LANES] for r in range(2)],
                                axis=0)
            s = lax.dot_general(q, kb, (((1,), (1,)), ((), ())), preferred_element_type=F32)
            probs, sink_terms = [], []
            for r in range(2):
                pr, sk = [], []
                for e in range(2):
                    sink = sink_ref[4 * kh + 2 * r + e] * LOG2E
                    se = s[r * Q_BLOCK:(r + 1) * Q_BLOCK, e * BAND:(e + 1) * BAND] + bias
                    mx = jnp.maximum(jnp.max(se, axis=1, keepdims=True), sink)
                    pr.append(jnp.exp2(se - mx).astype(BF16))
                    sk.append(jnp.exp2(sink - mx))
                probs.append(jnp.concatenate(pr, axis=1))
                sink_terms.append(jnp.where(out_first, sk[0], sk[1]))
            o = _dot(jnp.concatenate(probs, axis=0), vb_ext)
            for r in range(2):
                part = o[r * Q_BLOCK:(r + 1) * Q_BLOCK]
                den = part[:, LANES:] + sink_terms[r]
                o_ref[0, rows, (2 * kh + r) * LANES:(2 * kh + r + 1) * LANES] = (
                    part[:, :LANES] / den).astype(o_ref.dtype)


def _attention(q, kd, vd, sink):
    b, l, _ = q.shape
    tq = min(Q_TILE, l)
    return pl.pallas_call(
        functools.partial(_attn_kernel, seq=l),
        out_shape=jax.ShapeDtypeStruct((b, l, D_ATTN), BF16),
        grid=(b, l // tq),
        in_specs=[
            pl.BlockSpec(memory_space=pltpu.SMEM),
            pl.BlockSpec((1, tq, D_ATTN), lambda i, j: (i, j, 0)),
            pl.BlockSpec((1, l, KV_SLABS), lambda i, j: (i, 0, 0)),
            pl.BlockSpec((1, l, KV_SLABS), lambda i, j: (i, 0, 0)),
        ],
        out_specs=pl.BlockSpec((1, tq, D_ATTN), lambda i, j: (i, j, 0)),
        compiler_params=_params(("arbitrary", "arbitrary"), 32),
        name="window_attn",
    )(sink, q, kd, vd)


def _filter_kernel(z_ref, w1_ref, b1_ref, w2_ref, b2_ref, w3_ref, b3_ref, w4_ref, fr_ref, dec_ref,
                   e_ref, d_ref):
    z = z_ref[...]
    fr = fr_ref[...]
    h = jnp.sin(fr * (_dot3(z, w1_ref[...]) + b1_ref[...]))
    h = jnp.sin(fr * (_dot3(h, w2_ref[...]) + b2_ref[...]))
    h = jnp.sin(fr * (_dot3(h, w3_ref[...]) + b3_ref[...]))
    filt = _dot3(h, w4_ref[...]) * jnp.exp(-z[:, 0:1] * jnp.abs(dec_ref[...]))
    fwd, bwd = filt[:, :D_HY], filt[:, D_HY:]
    e_ref[...] = (fwd + bwd).astype(BF16)
    d_ref[...] = (fwd - bwd).astype(BF16)


def _filter_features(l):
    t = jnp.linspace(0.0, 1.0, l, dtype=F32)[:, None]
    n_bands = (FILTER_EMB - 1) // 2
    w = 2.0 * math.pi * jnp.arange(l, dtype=F32)[:, None] / l
    fb = jnp.linspace(1e-4, n_bands - 1, n_bands, dtype=F32)[None]
    z = jnp.concatenate([t, jnp.cos(fb * w), -jnp.sin(fb * w)], axis=-1)
    return jnp.pad(z, ((0, 0), (0, FILTER_HID - FILTER_EMB)))


def _filters(z, w1, b1, w2, b2, w3, b3, w4, freq, decay):
    l = z.shape[0]
    tl = min(ROW_TILE, l)
    w1p = jnp.pad(w1, ((0, FILTER_HID - FILTER_EMB), (0, 0)))
    vec = lambda a: a.reshape(1, -1)
    out = jax.ShapeDtypeStruct((l, D_HY), BF16)
    tile = pl.BlockSpec((tl, D_HY), lambda i: (i, 0))
    return pl.pallas_call(
        _filter_kernel,
        out_shape=(out, out),
        grid=(l // tl,),
        in_specs=[
            pl.BlockSpec((tl, FILTER_HID), lambda i: (i, 0)),
            _resident((FILTER_HID, FILTER_HID)), _resident((1, FILTER_HID)),
            _resident((FILTER_HID, FILTER_HID)), _resident((1, FILTER_HID)),
            _resident((FILTER_HID, FILTER_HID)), _resident((1, FILTER_HID)),
            _resident((FILTER_HID, 2 * D_HY)), _resident((1, FILTER_HID)), _resident((1, 2 * D_HY)),
        ],
        out_specs=(tile, tile),
        compiler_params=_params(("arbitrary",), 32),
        name="hyena_filter",
    )(z, w1p, vec(b1), w2, vec(b2), w3, vec(b3), w4, vec(freq), vec(decay))


def _fft_matrices(l):
    n = 2 * l
    n2 = FFT_N2
    n1 = n // n2
    k1 = jnp.arange(n1, dtype=jnp.int32)
    t1 = jnp.arange(n1 // 2, dtype=jnp.int32)
    ang1 = ((k1[:, None] * t1[None, :]) % n1).astype(F32) * (2.0 * math.pi / n1)
    f1 = jnp.stack([jnp.cos(ang1), jnp.sin(ang1)], axis=1).reshape(2 * n1, n1 // 2)
    eye = jnp.eye(ROW_GROUP, dtype=F32)
    kron = (f1[:, None, :, None] * eye[None, :, None, :]).reshape(2 * n1 * ROW_GROUP, (n1 // 2) * ROW_GROUP)
    k2 = jnp.arange(n2 // 2, dtype=jnp.int32)
    t2 = jnp.arange(n2, dtype=jnp.int32)
    k = k1[:, None] + n1 * k2[None, :]
    ang2 = ((k[:, :, None] * t2[None, None, :]) % n).astype(F32) * (2.0 * math.pi / n)
    c, s = jnp.cos(ang2), jnp.sin(ang2)
    re_rows = jnp.concatenate([c, -s], axis=2)
    im_rows = jnp.concatenate([s, c], axis=2)
    nyq = jnp.concatenate([jnp.where(t2 % 2 == 0, 1.0, -1.0).astype(F32), jnp.zeros((n2,), F32)])
    im_rows = im_rows.at[0, 0, :].set(nyq)
    g = jnp.concatenate([re_rows, im_rows], axis=1)
    return dict(kron=kron.astype(BF16), kron_t=kron.T.astype(BF16),
                g=g.astype(BF16), g_t=g.transpose(0, 2, 1).astype(BF16))


def _fft_stage1(x_ref, kron_ref, b_scr):
    n1h, n2, c = x_ref.shape

    def group(g, carry):
        grp = _row_group(g)
        x = x_ref[:, grp, :].reshape(n1h * ROW_GROUP, c)
        b_scr[:, grp, :] = _dot(kron_ref[...], x).astype(BF16).reshape(4 * n1h, ROW_GROUP, c)
        return carry

    lax.fori_loop(0, n2 // ROW_GROUP, group, 0, unroll=FFT_UNROLL)


def _row_group(g):
    return pl.ds(pl.multiple_of(g * ROW_GROUP, ROW_GROUP), ROW_GROUP)


def _fft_stage2(g_ref, b_scr, k):
    n2, c = b_scr.shape[1:]
    return _dot(g_ref[k], b_scr[pl.ds(2 * k, 2)].reshape(2 * n2, c))


def _spectrum_kernel(e_ref, d_ref, kron_ref, g_ref, ka_ref, kb_ref, ka2_ref, be_scr, bd_scr, *, seq):
    half = ka_ref.shape[1]
    _fft_stage1(e_ref, kron_ref, be_scr)
    _fft_stage1(d_ref, kron_ref, bd_scr)
    first_row = lax.broadcasted_iota(jnp.int32, (half, 1), 0) == 0

    def freq(k, carry):
        n2, c = be_scr.shape[1:]
        be = be_scr[pl.ds(2 * k, 2)].reshape(2 * n2, c)
        bd = bd_scr[pl.ds(2 * k, 2)].reshape(2 * n2, c)
        re = _dot(g_ref[k, :half, :], be)
        im = _dot(g_ref[k, half:, :], bd)
        nyq = _dot(g_ref[k, half:half + ROW_GROUP, :], be)[0:1]
        is_dc = first_row & (k == 0)
        ka = re * jnp.where(is_dc, 0.5 / seq, 1.0 / seq)
        ka_ref[k] = ka
        kb_ref[k] = jnp.where(is_dc, 0.0, im * (1.0 / seq))
        ka2_ref[k] = jnp.where(is_dc, nyq * (0.5 / seq), ka)
        return carry

    lax.fori_loop(0, ka_ref.shape[0], freq, 0, unroll=FFT_UNROLL)


def _spectrum(mats, e, d):
    l = e.shape[0]
    n1, n2 = mats["g"].shape[:2]
    cw = LANES
    out = jax.ShapeDtypeStruct((n1, n2 // 2, D_HY), F32)
    src = pl.BlockSpec((n1 // 2, n2, cw), lambda h: (0, 0, h))
    dst = pl.BlockSpec((n1, n2 // 2, cw), lambda h: (0, 0, h))
    view = lambda a: a.reshape(n1 // 2, n2, D_HY)
    return pl.pallas_call(
        functools.partial(_spectrum_kernel, seq=l),
        out_shape=(out, out, out),
        grid=(D_HY // cw,),
        in_specs=[src, src, _resident(mats["kron"].shape), _resident(mats["g"].shape)],
        out_specs=(dst, dst, dst),
        scratch_shapes=[pltpu.VMEM((2 * n1, n2, cw), BF16), pltpu.VMEM((2 * n1, n2, cw), BF16)],
        compiler_params=_params(("arbitrary",), 56),
        name="hyena_spectrum",
    )(view(e), view(d), mats["kron"], mats["g"])


def _long_conv_kernel(p_ref, x0_ref, kron_ref, kron_t_ref, g_ref, gt_ref, ka_ref, kb_ref, ka2_ref, bias_ref,
                      o_ref, b_scr):
    n1h, n2, c = p_ref.shape[1:]
    half = n2 // 2
    _fft_stage1(p_ref.at[0], kron_ref, b_scr)

    def freq(k, carry):
        u = _fft_stage2(g_ref, b_scr, k)
        ure, uim = u[:half], u[half:]
        ka, kb, ka2 = ka_ref[k], kb_ref[k], ka2_ref[k]
        y = jnp.concatenate([ure * ka - uim * kb, ure * kb + uim * ka2], axis=0).astype(BF16)
        b_scr[pl.ds(2 * k, 2)] = _dot(gt_ref[k], y).astype(BF16).reshape(2, n2, c)
        return carry

    lax.fori_loop(0, 2 * n1h, freq, 0, unroll=FFT_UNROLL)
    bias = bias_ref[...]

    def group(g, carry):
        grp = _row_group(g)
        z = b_scr[:, grp, :].reshape(4 * n1h * ROW_GROUP, c)
        y = _dot(kron_t_ref[...], z).reshape(n1h, ROW_GROUP, c)
        gated = (y + p_ref[0, :, grp, :].astype(F32) * bias) * x0_ref[0, :, grp, :].astype(F32)
        o_ref[0, :, grp, :] = gated.astype(BF16)
        return carry

    lax.fori_loop(0, n2 // ROW_GROUP, group, 0, unroll=FFT_UNROLL)


def _long_conv(p, x0, mats, ka, kb, ka2, bias):
    b, l, _ = p.shape
    n1, n2 = mats["g"].shape[:2]
    cw = FFT_LANES
    view = lambda a: a.reshape(b, n1 // 2, n2, D_HY)
    tile = pl.BlockSpec((1, n1 // 2, n2, cw), lambda h, i: (i, 0, 0, h))
    spec = pl.BlockSpec((n1, n2 // 2, cw), lambda h, i: (0, 0, h), pipeline_mode=pl.Buffered(1))
    out = pl.pallas_call(
        _long_conv_kernel,
        out_shape=jax.ShapeDtypeStruct((b, n1 // 2, n2, D_HY), BF16),
        grid=(D_HY // cw, b),
        in_specs=[
            tile, tile,
            _resident(mats["kron"].shape), _resident(mats["kron_t"].shape),
            _resident(mats["g"].shape), _resident(mats["g_t"].shape),
            spec, spec, spec,
            pl.BlockSpec((1, cw), lambda h, i: (0, h)),
        ],
        out_specs=tile,
        scratch_shapes=[pltpu.VMEM((2 * n1, n2, cw), BF16)],
        compiler_params=_params(("arbitrary", "arbitrary"), 60),
        name="hyena_long_conv",
    )(view(p), view(x0), mats["kron"], mats["kron_t"], mats["g"], mats["g_t"], ka, kb, ka2, bias.reshape(1, -1))
    return out.reshape(b, l, D_HY)


def _chunk_ffn_weights(wi, wo):
    nck = D_FF // FF_CHUNK
    g = wi[:, :D_FF].reshape(D_MODEL, nck, FF_CHUNK)
    u = wi[:, D_FF:].reshape(D_MODEL, nck, FF_CHUNK)
    wi_c = jnp.concatenate([g, u], axis=2).transpose(1, 0, 2).astype(BF16)
    return wi_c, wo.astype(BF16)


def _trunk(x, mod, weights, p):
    l = x.shape[1]
    depth = mod.shape[0]
    rope = _rope_tables(l)
    feats = _filter_features(l)
    mats = _fft_matrices(l)
    vec = lambda a: a.reshape(1, -1)
    for i in range(depth):
        w = weights[i]
        m = mod[i]
        x = _ffn(x, m, w["wi1"], w["wo1"], vec(p["ln_g"][i, 0]), vec(p["ln_b"][i, 0]))
        q, kd, vd, gated, x0 = _in_proj(x, m, w["w_in"], p["hy_conv_w"][i], p["hy_conv_b"][i], rope)
        o_attn = _attention(q, kd, vd, p["sink"][i])
        e, d = _filters(feats, p["hy_w1"][i], p["hy_b1"][i], p["hy_w2"][i], p["hy_b2"][i],
                        p["hy_w3"][i], p["hy_b3"][i], p["hy_w4"][i], p["hy_freq"][i], p["hy_decay"][i])
        ka, kb, ka2 = _spectrum(mats, e, d)
        o_hy = _long_conv(gated, x0, mats, ka, kb, ka2, p["hy_bias"][i])
        x = _mix_ffn(x, o_attn, o_hy, m, w["w_out"], vec(p["grp_norm_g"][i]), vec(p["ln_g"][i, 1]),
                     vec(p["ln_b"][i, 1]), w["wi2"], w["wo2"], vec(p["ln_g"][i, 2]), vec(p["ln_b"][i, 2]))
    return x


def kernel(x_prompt, x_sample, c_prompt, c_sample, ada_w, ada_b, ffn1_wi, ffn1_wo, ffn2_wi, ffn2_wo, ln_g, ln_b, w_in, w_out, sink, grp_norm_g, hy_conv_w, hy_conv_b, hy_w1, hy_b1, hy_w2, hy_b2, hy_w3, hy_b3, hy_w4, hy_freq, hy_decay, hy_bias):
    p = dict(ln_g=ln_g, ln_b=ln_b, sink=sink, grp_norm_g=grp_norm_g, hy_conv_w=hy_conv_w,
             hy_conv_b=hy_conv_b, hy_w1=hy_w1, hy_b1=hy_b1, hy_w2=hy_w2, hy_b2=hy_b2, hy_w3=hy_w3,
             hy_b3=hy_b3, hy_w4=hy_w4, hy_freq=hy_freq, hy_decay=hy_decay, hy_bias=hy_bias)
    depth = ada_w.shape[0]
    weights = []
    for i in range(depth):
        wi1, wo1 = _chunk_ffn_weights(ffn1_wi[i], ffn1_wo[i])
        wi2, wo2 = _chunk_ffn_weights(ffn2_wi[i], ffn2_wo[i])
        weights.append(dict(wi1=wi1, wo1=wo1, wi2=wi2, wo2=wo2,
                            w_in=w_in[i].astype(BF16), w_out=w_out[i].astype(BF16)))
    nb = c_prompt.shape[0]
    mod = _modulation(jnp.concatenate([c_prompt, c_sample], axis=0), ada_w, ada_b)
    y_prompt = _trunk(x_prompt, mod[:, :nb], weights, p)
    y_sample = _trunk(x_sample, mod[:, nb:], weights, p)
    return (y_prompt, y_sample)
```

```python
import functools
import math

import jax
import jax.numpy as jnp
from jax import lax
from jax.experimental import pallas as pl
from jax.experimental.pallas import tpu as pltpu

F32 = jnp.float32
BF16 = jnp.bfloat16

D_MODEL = 1024
D_ATTN = 512
D_HY = 512
HEAD_DIM = 64
N_Q_HEADS = D_ATTN // HEAD_DIM
N_KV_HEADS = 2
KV_WIDTH = N_KV_HEADS * HEAD_DIM
ROT_DIM = HEAD_DIM // 4
ROPE_THETA = 500000.0
WINDOW = 128
FILTER_EMB = 33
FILTER_HID = 64
D_FF = 2816
N_IN = D_ATTN + 2 * KV_WIDTH + 3 * D_HY
N_MOD = 9
DEPTH = 4
ALPHA = float((2 * DEPTH) ** 0.25)
LN_EPS = 1e-5
RMS_EPS = 1e-6
NEG_BIG = -1e30
LOG2E = math.log2(math.e)
Q_SCALE = HEAD_DIM ** -0.5 * LOG2E

LANES = 128
KV_SLABS = 2 * N_KV_HEADS * LANES
VMEM_BYTES = 64 * 1024 * 1024
MIB = 1024 * 1024

ROW_TILE = 1024
FFN_TILE = 1024
FFN_SUB = 512
FF_CHUNK = 256
LN_ROWS = 256
Q_TILE = 1024
Q_BLOCK = 128
BAND = 3 * Q_BLOCK
FFT_N2 = 256
FFT_LANES = 256
ROW_GROUP = 16
FFT_UNROLL = 8
MOD_N_TILE = 1536
HALO = 16
CONV_LANES = 256


def _params(semantics, vmem_mib):
    return pltpu.CompilerParams(dimension_semantics=semantics,
                                vmem_limit_bytes=min(vmem_mib * MIB, VMEM_BYTES - 4 * MIB))


def _resident(shape):
    nd = len(shape)
    return pl.BlockSpec(shape, lambda *_: (0,) * nd, pipeline_mode=pl.Buffered(1))


def _dot(a, b):
    return jnp.dot(a, b, preferred_element_type=F32)


def _split(a):
    hi = a.astype(BF16)
    lo = (a - hi.astype(F32)).astype(BF16)
    return hi, lo


def _dot3(a, b):
    a_hi, a_lo = _split(a)
    b_hi, b_lo = _split(b)
    return _dot(a_hi, b_hi) + _dot(a_hi, b_lo) + _dot(a_lo, b_hi)


def _layer_norm(y, g, b):
    mu = jnp.mean(y, axis=-1, keepdims=True)
    d = y - mu
    var = jnp.mean(d * d, axis=-1, keepdims=True)
    return d * lax.rsqrt(var + LN_EPS) * g + b


def _rms_norm(y, g):
    ms = jnp.mean(y * y, axis=-1, keepdims=True)
    return y * lax.rsqrt(ms + RMS_EPS) * g


def _mod_kernel(c_ref, w_ref, b_ref, o_ref):
    c = c_ref[...]
    a = c * jax.nn.sigmoid(c)
    o_ref[0] = _dot3(a, w_ref[0]) + b_ref[0]


def _modulation(c, ada_w, ada_b):
    depth = ada_w.shape[0]
    b = c.shape[0]
    n = N_MOD * D_MODEL
    out = pl.pallas_call(
        _mod_kernel,
        out_shape=jax.ShapeDtypeStruct((depth, b, n), F32),
        grid=(depth, n // MOD_N_TILE),
        in_specs=[
            pl.BlockSpec((b, D_MODEL), lambda l, j: (0, 0)),
            pl.BlockSpec((1, D_MODEL, MOD_N_TILE), lambda l, j: (l, 0, j)),
            pl.BlockSpec((1, 1, MOD_N_TILE), lambda l, j: (l, 0, j)),
        ],
        out_specs=pl.BlockSpec((1, b, MOD_N_TILE), lambda l, j: (l, 0, j)),
        compiler_params=_params(("arbitrary", "arbitrary"), 40),
        name="adaln_mod",
    )(c, ada_w, ada_b.reshape(depth, 1, n))
    return out.reshape(depth, b, N_MOD, D_MODEL)


def _ln_pieces():
    return [slice(r * LN_ROWS, (r + 1) * LN_ROWS) for r in range(FFN_SUB // LN_ROWS)]


def _modulate_rows(x, m, mod_row):
    return (x * (1.0 + m[mod_row + 1:mod_row + 2]) + m[mod_row:mod_row + 1]).astype(BF16)


def _swiglu_rows(x_rows, h_rows, a_rows, o_rows, gate, wi_ref, wo_ref, g_ref, b_ref):
    for c in range(D_FF // FF_CHUNK):
        gu = _dot(h_rows[...], wi_ref[c])
        g, u = gu[:, :FF_CHUNK], gu[:, FF_CHUNK:]
        a_rows[:, c * FF_CHUNK:(c + 1) * FF_CHUNK] = (g * jax.nn.sigmoid(g) * u).astype(BF16)
    half_gate = 0.5 * (1.0 + gate)
    for piece in _ln_pieces():
        y = ALPHA * x_rows[piece, :] + half_gate * _dot(a_rows[piece, :], wo_ref[...])
        o_rows[piece, :] = _layer_norm(y, g_ref[...], b_ref[...])


def _ffn_kernel(x_ref, mod_ref, wi_ref, wo_ref, g_ref, b_ref, o_ref, h_scr, a_scr):
    m = mod_ref[0]
    for s in range(h_scr.shape[0]):
        rows = pl.ds(s * FFN_SUB, FFN_SUB)
        for piece in _ln_pieces():
            h_scr[s, piece, :] = _modulate_rows(x_ref[0, pl.ds(s * FFN_SUB + piece.start, LN_ROWS), :], m, 0)
        _swiglu_rows(x_ref.at[0, rows], h_scr.at[s], a_scr.at[s], o_ref.at[0, rows], m[2:3],
                     wi_ref, wo_ref, g_ref, b_ref)


def _mix_ffn_kernel(x_ref, oa_ref, oh_ref, mod_ref, w_out_ref, gn_ref, gm_ref, bm_ref, wi_ref, wo_ref, g_ref, b_ref,
                    o_ref, h_scr, a_scr, x_scr):
    m = mod_ref[0]
    gn = gn_ref[...]
    for s in range(h_scr.shape[0]):
        for piece in _ln_pieces():
            rows = pl.ds(s * FFN_SUB + piece.start, LN_ROWS)
            a = _rms_norm(oa_ref[0, rows, :].astype(F32), gn[:, :D_ATTN]).astype(BF16)
            h = _rms_norm(oh_ref[0, rows, :].astype(F32), gn[:, D_ATTN:]).astype(BF16)
            o = _dot(a, w_out_ref[:D_ATTN, :]) + _dot(h, w_out_ref[D_ATTN:, :])
            x = _layer_norm(ALPHA * x_ref[0, rows, :] + (1.0 + m[5:6]) * o, gm_ref[...], bm_ref[...])
            x_scr[s, piece, :] = x
            h_scr[s, piece, :] = _modulate_rows(x, m, 6)
        _swiglu_rows(x_scr.at[s], h_scr.at[s], a_scr.at[s], o_ref.at[0, pl.ds(s * FFN_SUB, FFN_SUB)], m[8:9],
                     wi_ref, wo_ref, g_ref, b_ref)


def _ffn(x, mod, wi_c, wo, ln_g, ln_b, mix=None):
    b, l, _ = x.shape
    tm = min(FFN_TILE, l)
    nsub = tm // FFN_SUB
    nck = D_FF // FF_CHUNK
    row = lambda i, j: (i, j, 0)
    vec = _resident((1, D_MODEL))
    x_spec = pl.BlockSpec((1, tm, D_MODEL), row)
    mod_spec = pl.BlockSpec((1, N_MOD, D_MODEL), lambda i, j: (i, 0, 0))
    ffn_specs = [_resident((nck, D_MODEL, 2 * FF_CHUNK)), _resident((D_FF, D_MODEL)), vec, vec]
    scratch = [pltpu.VMEM((nsub, FFN_SUB, D_MODEL), BF16), pltpu.VMEM((nsub, FFN_SUB, D_FF), BF16)]
    if mix is None:
        body, name = _ffn_kernel, "ffn"
        in_specs = [x_spec, mod_spec] + ffn_specs
        args = (x, mod, wi_c, wo, ln_g, ln_b)
    else:
        body, name = _mix_ffn_kernel, "mix_ffn"
        half = pl.BlockSpec((1, tm, D_ATTN), row)
        in_specs = [x_spec, half, half, mod_spec, _resident((D_MODEL, D_MODEL)), vec, vec, vec] + ffn_specs
        args = (x, mix[0], mix[1], mod, *mix[2:], wi_c, wo, ln_g, ln_b)
        scratch.append(pltpu.VMEM((nsub, FFN_SUB, D_MODEL), F32))
    return pl.pallas_call(
        body,
        out_shape=jax.ShapeDtypeStruct(x.shape, F32),
        grid=(b, l // tm),
        in_specs=in_specs,
        out_specs=x_spec,
        scratch_shapes=scratch,
        compiler_params=_params(("arbitrary", "arbitrary"), 60),
        name=name,
    )(*args)


def _rope_slab(t, cos, sin_up, sin_dn):
    return (t * cos + pltpu.roll(t, ROT_DIM // 2, 1) * sin_up
            + pltpu.roll(t, LANES - ROT_DIM // 2, 1) * sin_dn)


def _in_proj_kernel(x_ref, xp_ref, xn_ref, mod_ref, w_ref, cw_ref, cb_ref, cos_ref, sup_ref, sdn_ref,
                    q_ref, kd_ref, vd_ref, p_ref, x0_ref):
    j = pl.program_id(1)
    tm = x_ref.shape[1]
    m = mod_ref[0]
    x = jnp.concatenate([xp_ref[0], x_ref[0], xn_ref[0]], axis=0)
    h = (x * (1.0 + m[4:5]) + m[3:4]).astype(BF16)
    inner = slice(HALO, HALO + tm)
    first_tile, last_tile = j == 0, j == pl.num_programs(1) - 1

    def conv(col):
        lanes = slice(col, col + CONV_LANES)
        z = _dot(h, w_ref[:, D_ATTN + 2 * KV_WIDTH + col:D_ATTN + 2 * KV_WIDTH + col + CONV_LANES])
        z = jnp.concatenate([jnp.where(first_tile, 0.0, z[:HALO]), z[inner],
                             jnp.where(last_tile, 0.0, z[HALO + tm:])], axis=0)
        return (pltpu.roll(z, 1, 0)[inner] * cw_ref[0:1, lanes] + z[inner] * cw_ref[1:2, lanes]
                + pltpu.roll(z, tm + 2 * HALO - 1, 0)[inner] * cw_ref[2:3, lanes] + cb_ref[:, lanes])

    z = _dot(h[inner], w_ref[:, :D_ATTN + 2 * KV_WIDTH])
    cos, sup, sdn = cos_ref[...], sup_ref[...], sdn_ref[...]
    for s in range(D_ATTN // LANES):
        q = _rope_slab(z[:, s * LANES:(s + 1) * LANES], cos, sup, sdn)
        q_ref[0, :, s * LANES:(s + 1) * LANES] = (q * Q_SCALE).astype(BF16)
    k = _rope_slab(z[:, D_ATTN:D_ATTN + KV_WIDTH], cos, sup, sdn)
    v = z[:, D_ATTN + KV_WIDTH:D_ATTN + 2 * KV_WIDTH]
    lane = lax.broadcasted_iota(jnp.int32, k.shape, 1)
    first = lane < HEAD_DIM
    for t, ref in ((k, kd_ref), (v, vd_ref)):
        swapped = pltpu.roll(t, HEAD_DIM, 1)
        slabs = (jnp.where(first, t, 0.0), jnp.where(first, 0.0, swapped),
                 jnp.where(first, swapped, 0.0), jnp.where(first, 0.0, t))
        for i, slab in enumerate(slabs):
            ref[0, :, i * LANES:(i + 1) * LANES] = slab.astype(BF16)
    for t in range(D_HY // CONV_LANES):
        lanes = slice(t * CONV_LANES, (t + 1) * CONV_LANES)
        x0, x1, hv = (conv(g * D_HY + t * CONV_LANES) for g in range(3))
        p_ref[0, :, lanes] = (hv * x1).astype(BF16)
        x0_ref[0, :, lanes] = x0.astype(BF16)


def _in_proj(x, mod, w_in, conv_w, conv_b, rope):
    b, l, _ = x.shape
    tm = min(ROW_TILE, l)
    per = tm // HALO
    last = l // HALO - 1
    row = lambda i, j: (i, j, 0)
    tab = pl.BlockSpec((tm, LANES), lambda i, j: (j, 0))
    return pl.pallas_call(
        _in_proj_kernel,
        out_shape=(
            jax.ShapeDtypeStruct((b, l, D_ATTN), BF16),
            jax.ShapeDtypeStruct((b, l, KV_SLABS), BF16),
            jax.ShapeDtypeStruct((b, l, KV_SLABS), BF16),
            jax.ShapeDtypeStruct((b, l, D_HY), BF16),
            jax.ShapeDtypeStruct((b, l, D_HY), BF16),
        ),
        grid=(b, l // tm),
        in_specs=[
            pl.BlockSpec((1, tm, D_MODEL), row),
            pl.BlockSpec((1, HALO, D_MODEL), lambda i, j: (i, jnp.maximum(j * per - 1, 0), 0)),
            pl.BlockSpec((1, HALO, D_MODEL), lambda i, j: (i, jnp.minimum((j + 1) * per, last), 0)),
            pl.BlockSpec((1, N_MOD, D_MODEL), lambda i, j: (i, 0, 0)),
            _resident((D_MODEL, N_IN)),
            _resident((3, 3 * D_HY)),
            _resident((1, 3 * D_HY)),
            tab, tab, tab,
        ],
        out_specs=(
            pl.BlockSpec((1, tm, D_ATTN), row),
            pl.BlockSpec((1, tm, KV_SLABS), row),
            pl.BlockSpec((1, tm, KV_SLABS), row),
            pl.BlockSpec((1, tm, D_HY), row),
            pl.BlockSpec((1, tm, D_HY), row),
        ),
        compiler_params=_params(("arbitrary", "arbitrary"), 48),
        name="in_proj",
    )(x, x, x, mod, w_in, conv_w, conv_b.reshape(1, -1), *rope)


def _rope_tables(l):
    half = ROT_DIM // 2
    inv = ROPE_THETA ** (-jnp.arange(0, ROT_DIM, 2, dtype=F32) / ROT_DIM)
    ang = jnp.arange(l, dtype=F32)[:, None] * inv[None]
    cos, sin = jnp.cos(ang), jnp.sin(ang)
    rest = HEAD_DIM - ROT_DIM
    cos_h = jnp.concatenate([cos, cos, jnp.ones((l, rest), F32)], axis=1)
    up_h = jnp.concatenate([jnp.zeros((l, half), F32), sin, jnp.zeros((l, rest), F32)], axis=1)
    dn_h = jnp.concatenate([-sin, jnp.zeros((l, half + rest), F32)], axis=1)
    reps = LANES // HEAD_DIM
    return tuple(jnp.tile(t, (1, reps)) for t in (cos_h, up_h, dn_h))


def _attn_kernel(sink_ref, q_ref, kd_ref, vd_ref, o_ref, *, seq):
    j = pl.program_id(1)
    first = lax.broadcasted_iota(jnp.int32, (BAND, LANES), 1) < HEAD_DIM
    out_first = lax.broadcasted_iota(jnp.int32, (Q_BLOCK, LANES), 1) < HEAD_DIM
    ones_ext = jnp.concatenate([jnp.where(first, 1.0, 0.0), jnp.where(first, 0.0, 1.0)], axis=0).astype(BF16)
    for blk in range(q_ref.shape[1] // Q_BLOCK):
        q0 = j * q_ref.shape[1] + blk * Q_BLOCK
        start = pl.multiple_of(jnp.clip(q0 - Q_BLOCK, 0, seq - BAND), Q_BLOCK)
        qpos = q0 + lax.broadcasted_iota(jnp.int32, (Q_BLOCK, BAND), 0)
        kpos = start + lax.broadcasted_iota(jnp.int32, (Q_BLOCK, BAND), 1)
        bias = jnp.where(jnp.abs(qpos - kpos) <= WINDOW, 0.0, NEG_BIG).astype(F32)
        rows = slice(blk * Q_BLOCK, (blk + 1) * Q_BLOCK)
        for kh in range(N_KV_HEADS):
            kb, vb = (jnp.concatenate([ref[0, pl.ds(start, BAND), (2 * kh + e) * LANES:(2 * kh + e + 1) * LANES]
                                       for e in range(2)], axis=0) for ref in (kd_ref, vd_ref))
            vb_ext = jnp.concatenate([vb, ones_ext], axis=1)
            q = jnp.concatenate([q_ref[0, rows, (2 * kh + r) * LANES:(2 * kh + r + 1) * LANES] for r in range(2)],
                                axis=0)
            s = lax.dot_general(q, kb, (((1,), (1,)), ((), ())), preferred_element_type=F32)
            probs, sink_terms = [], []
            for r in range(2):
                pr, sk = [], []
                for e in range(2):
                    sink = sink_ref[4 * kh + 2 * r + e] * LOG2E
                    se = s[r * Q_BLOCK:(r + 1) * Q_BLOCK, e * BAND:(e + 1) * BAND] + bias
                    mx = jnp.maximum(jnp.max(se, axis=1, keepdims=True), sink)
                    pr.append(jnp.exp2(se - mx).astype(BF16))
                    sk.append(jnp.exp2(sink - mx))
                probs.append(jnp.concatenate(pr, axis=1))
                sink_terms.append(jnp.where(out_first, sk[0], sk[1]))
            o = _dot(jnp.concatenate(probs, axis=0), vb_ext)
            for r in range(2):
                part = o[r * Q_BLOCK:(r + 1) * Q_BLOCK]
                den = part[:, LANES:] + sink_terms[r]
                o_ref[0, rows, (2 * kh + r) * LANES:(2 * kh + r + 1) * LANES] = (
                    part[:, :LANES] / den).astype(o_ref.dtype)


def _attention(q, kd, vd, sink):
    b, l, _ = q.shape
    tq = min(Q_TILE, l)
    return pl.pallas_call(
        functools.partial(_attn_kernel, seq=l),
        out_shape=jax.ShapeDtypeStruct((b, l, D_ATTN), BF16),
        grid=(b, l // tq),
        in_specs=[
            pl.BlockSpec(memory_space=pltpu.SMEM),
            pl.BlockSpec((1, tq, D_ATTN), lambda i, j: (i, j, 0)),
            pl.BlockSpec((1, l, KV_SLABS), lambda i, j: (i, 0, 0)),
            pl.BlockSpec((1, l, KV_SLABS), lambda i, j: (i, 0, 0)),
        ],
        out_specs=pl.BlockSpec((1, tq, D_ATTN), lambda i, j: (i, j, 0)),
        compiler_params=_params(("arbitrary", "arbitrary"), 32),
        name="window_attn",
    )(sink, q, kd, vd)


def _filter_kernel(z_ref, w1_ref, b1_ref, w2_ref, b2_ref, w3_ref, b3_ref, w4_ref, fr_ref, dec_ref,
                   e_ref, d_ref):
    z = z_ref[...]
    fr = fr_ref[...]
    h = jnp.sin(fr * (_dot3(z, w1_ref[...]) + b1_ref[...]))
    h = jnp.sin(fr * (_dot3(h, w2_ref[...]) + b2_ref[...]))
    h = jnp.sin(fr * (_dot3(h, w3_ref[...]) + b3_ref[...]))
    filt = _dot3(h, w4_ref[...]) * jnp.exp(-z[:, 0:1] * jnp.abs(dec_ref[...]))
    fwd, bwd = filt[:, :D_HY], filt[:, D_HY:]
    e_ref[...] = (fwd + bwd).astype(BF16)
    d_ref[...] = (fwd - bwd).astype(BF16)


def _filter_features(l):
    t = jnp.linspace(0.0, 1.0, l, dtype=F32)[:, None]
    n_bands = (FILTER_EMB - 1) // 2
    w = 2.0 * math.pi * jnp.arange(l, dtype=F32)[:, None] / l
    fb = jnp.linspace(1e-4, n_bands - 1, n_bands, dtype=F32)[None]
    z = jnp.concatenate([t, jnp.cos(fb * w), -jnp.sin(fb * w)], axis=-1)
    return jnp.pad(z, ((0, 0), (0, FILTER_HID - FILTER_EMB)))


def _filters(z, w1, b1, w2, b2, w3, b3, w4, freq, decay):
    l = z.shape[0]
    tl = min(ROW_TILE, l)
    w1p = jnp.pad(w1, ((0, FILTER_HID - FILTER_EMB), (0, 0)))
    vec = lambda a: a.reshape(1, -1)
    out = jax.ShapeDtypeStruct((l, D_HY), BF16)
    tile = pl.BlockSpec((tl, D_HY), lambda i: (i, 0))
    return pl.pallas_call(
        _filter_kernel,
        out_shape=(out, out),
        grid=(l // tl,),
        in_specs=[
            pl.BlockSpec((tl, FILTER_HID), lambda i: (i, 0)),
            _resident((FILTER_HID, FILTER_HID)), _resident((1, FILTER_HID)),
            _resident((FILTER_HID, FILTER_HID)), _resident((1, FILTER_HID)),
            _resident((FILTER_HID, FILTER_HID)), _resident((1, FILTER_HID)),
            _resident((FILTER_HID, 2 * D_HY)), _resident((1, FILTER_HID)), _resident((1, 2 * D_HY)),
        ],
        out_specs=(tile, tile),
        compiler_params=_params(("arbitrary",), 32),
        name="hyena_filter",
    )(z, w1p, vec(b1), w2, vec(b2), w3, vec(b3), w4, vec(freq), vec(decay))


def _fft_matrices(l):
    n = 2 * l
    n2 = FFT_N2
    n1 = n // n2
    kk = n1 // 2 + 1
    kh = jnp.arange(kk, dtype=jnp.int32)
    t1 = jnp.arange(n1 // 2, dtype=jnp.int32)
    ang1 = ((kh[:, None] * t1[None, :]) % n1).astype(F32) * (2.0 * math.pi / n1)
    f1 = jnp.stack([jnp.cos(ang1), jnp.sin(ang1)], axis=1).reshape(2 * kk, n1 // 2)
    eye = jnp.eye(ROW_GROUP, dtype=F32)
    kron = (f1[:, None, :, None] * eye[None, :, None, :]).reshape(2 * kk * ROW_GROUP, (n1 // 2) * ROW_GROUP)
    k1 = jnp.arange(n1, dtype=jnp.int32)
    k2 = jnp.arange(n2 // 2, dtype=jnp.int32)
    t2 = jnp.arange(n2, dtype=jnp.int32)
    k = k1[:, None] + n1 * k2[None, :]
    ang2 = ((k[:, :, None] * t2[None, None, :]) % n).astype(F32) * (2.0 * math.pi / n)
    c, s = jnp.cos(ang2), jnp.sin(ang2)
    re_rows = jnp.concatenate([c, -s], axis=2)
    im_rows = jnp.concatenate([s, c], axis=2)
    nyq = jnp.concatenate([jnp.where(t2 % 2 == 0, 1.0, -1.0).astype(F32), jnp.zeros((n2,), F32)])
    im_rows = im_rows.at[0, 0, :].set(nyq)
    conj = jnp.concatenate([jnp.ones((n2,), F32), -jnp.ones((n2,), F32)])
    mate = (n1 - kh) % n1
    has_mate = ((kh > 0) & (kh < n1 // 2))[:, None, None]
    re_mate = jnp.where(has_mate, re_rows[mate] * conj, 0.0)
    im_mate = jnp.where(has_mate, im_rows[mate] * conj, 0.0)
    g = jnp.concatenate([re_rows[:kk], re_mate, im_rows[:kk], im_mate], axis=1)
    return dict(kron=kron.astype(BF16), kron_t=kron.T.astype(BF16),
                g=g.astype(BF16), g_t=g.transpose(0, 2, 1).astype(BF16))


def _fft_stage1(x_ref, kron_ref, b_scr):
    n1h, n2, c = x_ref.shape

    def group(g, carry):
        grp = _row_group(g)
        x = x_ref[:, grp, :].reshape(n1h * ROW_GROUP, c)
        b_scr[:, grp, :] = _dot(kron_ref[...], x).astype(BF16).reshape(b_scr.shape[0], ROW_GROUP, c)
        return carry

    lax.fori_loop(0, n2 // ROW_GROUP, group, 0, unroll=FFT_UNROLL)


def _row_group(g):
    return pl.ds(pl.multiple_of(g * ROW_GROUP, ROW_GROUP), ROW_GROUP)


def _fft_stage2(g_ref, b_scr, k):
    n2, c = b_scr.shape[1:]
    return _dot(g_ref[k], b_scr[pl.ds(2 * k, 2)].reshape(2 * n2, c))


def _spectrum_kernel(e_ref, d_ref, kron_ref, g_ref, ka_ref, kb_ref, ka2_ref, be_scr, bd_scr, *, seq):
    half = ka_ref.shape[1]
    _fft_stage1(e_ref, kron_ref, be_scr)
    _fft_stage1(d_ref, kron_ref, bd_scr)
    first_row = lax.broadcasted_iota(jnp.int32, (half, 1), 0) == 0

    def freq(k, carry):
        n2, c = be_scr.shape[1:]
        be = be_scr[pl.ds(2 * k, 2)].reshape(2 * n2, c)
        bd = bd_scr[pl.ds(2 * k, 2)].reshape(2 * n2, c)
        re = _dot(g_ref[k, :half, :], be)
        im = _dot(g_ref[k, half:, :], bd)
        nyq = _dot(g_ref[k, half:half + ROW_GROUP, :], be)[0:1]
        is_dc = first_row & (k == 0)
        ka = re * jnp.where(is_dc, 0.5 / seq, 1.0 / seq)
        ka_ref[k] = ka
        kb_ref[k] = jnp.where(is_dc, 0.0, im * (1.0 / seq))
        ka2_ref[k] = jnp.where(is_dc, nyq * (0.5 / seq), ka)
        return carry

    lax.fori_loop(0, ka_ref.shape[0], freq, 0, unroll=FFT_UNROLL)


def _spectrum(mats, e, d):
    l = e.shape[0]
    kk, n2 = mats["g"].shape[0], FFT_N2
    n1h = l // n2
    cw = LANES
    out = jax.ShapeDtypeStruct((kk, n2, D_HY), F32)
    src = pl.BlockSpec((n1h, n2, cw), lambda h: (0, 0, h))
    dst = pl.BlockSpec((kk, n2, cw), lambda h: (0, 0, h))
    view = lambda a: a.reshape(n1h, n2, D_HY)
    return pl.pallas_call(
        functools.partial(_spectrum_kernel, seq=l),
        out_shape=(out, out, out),
        grid=(D_HY // cw,),
        in_specs=[src, src, _resident(mats["kron"].shape), _resident(mats["g"].shape)],
        out_specs=(dst, dst, dst),
        scratch_shapes=[pltpu.VMEM((2 * kk, n2, cw), BF16), pltpu.VMEM((2 * kk, n2, cw), BF16)],
        compiler_params=_params(("arbitrary",), 56),
        name="hyena_spectrum",
    )(view(e), view(d), mats["kron"], mats["g"])


def _long_conv_kernel(p_ref, x0_ref, kron_ref, kron_t_ref, g_ref, gt_ref, ka_ref, kb_ref, ka2_ref, bias_ref,
                      o_ref, b_scr):
    n1h, n2, c = p_ref.shape[1:]
    half = g_ref.shape[1] // 2
    _fft_stage1(p_ref.at[0], kron_ref, b_scr)

    def freq(k, carry):
        u = _fft_stage2(g_ref, b_scr, k)
        ure, uim = u[:half], u[half:]
        ka, kb, ka2 = ka_ref[k], kb_ref[k], ka2_ref[k]
        y = jnp.concatenate([ure * ka - uim * kb, ure * kb + uim * ka2], axis=0).astype(BF16)
        b_scr[pl.ds(2 * k, 2)] = _dot(gt_ref[k], y).astype(BF16).reshape(2, n2, c)
        return carry

    lax.fori_loop(0, g_ref.shape[0], freq, 0, unroll=FFT_UNROLL)
    bias = bias_ref[...]

    def group(g, carry):
        grp = _row_group(g)
        z = b_scr[:, grp, :].reshape(b_scr.shape[0] * ROW_GROUP, c)
        y = _dot(kron_t_ref[...], z).reshape(n1h, ROW_GROUP, c)
        gated = (y + p_ref[0, :, grp, :].astype(F32) * bias) * x0_ref[0, :, grp, :].astype(F32)
        o_ref[0, :, grp, :] = gated.astype(BF16)
        return carry

    lax.fori_loop(0, n2 // ROW_GROUP, group, 0, unroll=FFT_UNROLL)


def _long_conv(p, x0, mats, ka, kb, ka2, bias):
    b, l, _ = p.shape
    kk, n2 = mats["g"].shape[0], FFT_N2
    n1h = l // n2
    cw = FFT_LANES
    view = lambda a: a.reshape(b, n1h, n2, D_HY)
    tile = pl.BlockSpec((1, n1h, n2, cw), lambda h, i: (i, 0, 0, h))
    spec = pl.BlockSpec((kk, n2, cw), lambda h, i: (0, 0, h), pipeline_mode=pl.Buffered(1))
    out = pl.pallas_call(
        _long_conv_kernel,
        out_shape=jax.ShapeDtypeStruct((b, n1h, n2, D_HY), BF16),
        grid=(D_HY // cw, b),
        in_specs=[
            tile, tile,
            _resident(mats["kron"].shape), _resident(mats["kron_t"].shape),
            _resident(mats["g"].shape), _resident(mats["g_t"].shape),
            spec, spec, spec,
            pl.BlockSpec((1, cw), lambda h, i: (0, h)),
        ],
        out_specs=tile,
        scratch_shapes=[pltpu.VMEM((2 * kk, n2, cw), BF16)],
        compiler_params=_params(("arbitrary", "arbitrary"), 60),
        name="hyena_long_conv",
    )(view(p), view(x0), mats["kron"], mats["kron_t"], mats["g"], mats["g_t"], ka, kb, ka2, bias.reshape(1, -1))
    return out.reshape(b, l, D_HY)


def _chunk_ffn_weights(wi, wo):
    nck = D_FF // FF_CHUNK
    g = wi[:, :D_FF].reshape(D_MODEL, nck, FF_CHUNK)
    u = wi[:, D_FF:].reshape(D_MODEL, nck, FF_CHUNK)
    wi_c = jnp.concatenate([g, u], axis=2).transpose(1, 0, 2).astype(BF16)
    return wi_c, wo.astype(BF16)


def _trunk(x, mod, weights, p):
    l = x.shape[1]
    depth = mod.shape[0]
    rope = _rope_tables(l)
    feats = _filter_features(l)
    mats = _fft_matrices(l)
    vec = lambda a: a.reshape(1, -1)
    for i in range(depth):
        w = weights[i]
        m = mod[i]
        x = _ffn(x, m, w["wi1"], w["wo1"], vec(p["ln_g"][i, 0]), vec(p["ln_b"][i, 0]))
        q, kd, vd, gated, x0 = _in_proj(x, m, w["w_in"], p["hy_conv_w"][i], p["hy_conv_b"][i], rope)
        o_attn = _attention(q, kd, vd, p["sink"][i])
        e, d = _filters(feats, p["hy_w1"][i], p["hy_b1"][i], p["hy_w2"][i], p["hy_b2"][i],
                        p["hy_w3"][i], p["hy_b3"][i], p["hy_w4"][i], p["hy_freq"][i], p["hy_decay"][i])
        ka, kb, ka2 = _spectrum(mats, e, d)
        o_hy = _long_conv(gated, x0, mats, ka, kb, ka2, p["hy_bias"][i])
        mix = (o_attn, o_hy, w["w_out"], vec(p["grp_norm_g"][i]), vec(p["ln_g"][i, 1]), vec(p["ln_b"][i, 1]))
        x = _ffn(x, m, w["wi2"], w["wo2"], vec(p["ln_g"][i, 2]), vec(p["ln_b"][i, 2]), mix)
    return x


def kernel(x_prompt, x_sample, c_prompt, c_sample, ada_w, ada_b, ffn1_wi, ffn1_wo, ffn2_wi, ffn2_wo, ln_g, ln_b, w_in, w_out, sink, grp_norm_g, hy_conv_w, hy_conv_b, hy_w1, hy_b1, hy_w2, hy_b2, hy_w3, hy_b3, hy_w4, hy_freq, hy_decay, hy_bias):
    p = dict(ln_g=ln_g, ln_b=ln_b, sink=sink, grp_norm_g=grp_norm_g, hy_conv_w=hy_conv_w,
             hy_conv_b=hy_conv_b, hy_w1=hy_w1, hy_b1=hy_b1, hy_w2=hy_w2, hy_b2=hy_b2, hy_w3=hy_w3,
             hy_b3=hy_b3, hy_w4=hy_w4, hy_freq=hy_freq, hy_decay=hy_decay, hy_bias=hy_bias)
    depth = ada_w.shape[0]
    weights = []
    for i in range(depth):
        wi1, wo1 = _chunk_ffn_weights(ffn1_wi[i], ffn1_wo[i])
        wi2, wo2 = _chunk_ffn_weights(ffn2_wi[i], ffn2_wo[i])
        weights.append(dict(wi1=wi1, wo1=wo1, wi2=wi2, wo2=wo2,
                            w_in=w_in[i].astype(BF16), w_out=w_out[i].astype(BF16)))
    nb = c_prompt.shape[0]
    mod = _modulation(jnp.concatenate([c_prompt, c_sample], axis=0), ada_w, ada_b)
    y_prompt = _trunk(x_prompt, mod[:, :nb], weights, p)
    y_sample = _trunk(x_sample, mod[:, nb:], weights, p)
    return (y_prompt, y_sample)
```

```python
import functools
import math

import jax
import jax.numpy as jnp
from jax import lax
from jax.experimental import pallas as pl
from jax.experimental.pallas import tpu as pltpu

F32 = jnp.float32
BF16 = jnp.bfloat16

D_MODEL = 1024
D_ATTN = 512
D_HY = 512
HEAD_DIM = 64
N_Q_HEADS = D_ATTN // HEAD_DIM
N_KV_HEADS = 2
KV_WIDTH = N_KV_HEADS * HEAD_DIM
ROT_DIM = HEAD_DIM // 4
ROPE_THETA = 500000.0
WINDOW = 128
FILTER_EMB = 33
FILTER_HID = 64
D_FF = 2816
N_IN = D_ATTN + 2 * KV_WIDTH + 3 * D_HY
N_MOD = 9
DEPTH = 4
ALPHA = float((2 * DEPTH) ** 0.25)
LN_EPS = 1e-5
RMS_EPS = 1e-6
NEG_BIG = -1e30
LOG2E = math.log2(math.e)
Q_SCALE = HEAD_DIM ** -0.5 * LOG2E

LANES = 128
KV_SLABS = 2 * N_KV_HEADS * LANES
VMEM_BYTES = 64 * 1024 * 1024
MIB = 1024 * 1024

ROW_TILE = 1024
FFN_TILE = 1024
FFN_SUB = 512
FF_CHUNK = 256
LN_ROWS = 256
INTERLEAVE_EVERY = 2
Q_TILE = 1024
Q_BLOCK = 128
BAND = 3 * Q_BLOCK
FFT_N2 = 256
FFT_LANES = 256
ROW_GROUP = 16
FFT_UNROLL = 8
MOD_N_TILE = 1536
HALO = 16
CONV_LANES = 256


def _params(semantics, vmem_mib):
    return pltpu.CompilerParams(dimension_semantics=semantics,
                                vmem_limit_bytes=min(vmem_mib * MIB, VMEM_BYTES - 4 * MIB))


def _resident(shape):
    nd = len(shape)
    return pl.BlockSpec(shape, lambda *_: (0,) * nd, pipeline_mode=pl.Buffered(1))


def _dot(a, b):
    return jnp.dot(a, b, preferred_element_type=F32)


def _split(a):
    hi = a.astype(BF16)
    lo = (a - hi.astype(F32)).astype(BF16)
    return hi, lo


def _dot3(a, b):
    a_hi, a_lo = _split(a)
    b_hi, b_lo = _split(b)
    return _dot(a_hi, b_hi) + _dot(a_hi, b_lo) + _dot(a_lo, b_hi)


def _layer_norm(y, g, b):
    mu = jnp.mean(y, axis=-1, keepdims=True)
    d = y - mu
    var = jnp.mean(d * d, axis=-1, keepdims=True)
    return d * lax.rsqrt(var + LN_EPS) * g + b


def _rms_norm(y, g):
    ms = jnp.mean(y * y, axis=-1, keepdims=True)
    return y * lax.rsqrt(ms + RMS_EPS) * g


def _mod_kernel(c_ref, w_ref, b_ref, o_ref):
    c = c_ref[...]
    a = c * jax.nn.sigmoid(c)
    o_ref[0] = _dot3(a, w_ref[0]) + b_ref[0]


def _modulation(c, ada_w, ada_b):
    depth = ada_w.shape[0]
    b = c.shape[0]
    n = N_MOD * D_MODEL
    out = pl.pallas_call(
        _mod_kernel,
        out_shape=jax.ShapeDtypeStruct((depth, b, n), F32),
        grid=(depth, n // MOD_N_TILE),
        in_specs=[
            pl.BlockSpec((b, D_MODEL), lambda l, j: (0, 0)),
            pl.BlockSpec((1, D_MODEL, MOD_N_TILE), lambda l, j: (l, 0, j)),
            pl.BlockSpec((1, 1, MOD_N_TILE), lambda l, j: (l, 0, j)),
        ],
        out_specs=pl.BlockSpec((1, b, MOD_N_TILE), lambda l, j: (l, 0, j)),
        compiler_params=_params(("arbitrary", "arbitrary"), 40),
        name="adaln_mod",
    )(c, ada_w, ada_b.reshape(depth, 1, n))
    return out.reshape(depth, b, N_MOD, D_MODEL)


def _ln_pieces():
    return [slice(r * LN_ROWS, (r + 1) * LN_ROWS) for r in range(FFN_SUB // LN_ROWS)]


def _modulate_rows(x, m, mod_row):
    return (x * (1.0 + m[mod_row + 1:mod_row + 2]) + m[mod_row:mod_row + 1]).astype(BF16)


def _swiglu_up(h_rows, a_rows, wi_ref, interleave):
    pending = list(interleave)
    for c in range(D_FF // FF_CHUNK):
        gu = _dot(h_rows[...], wi_ref[c])
        g, u = gu[:, :FF_CHUNK], gu[:, FF_CHUNK:]
        a_rows[:, c * FF_CHUNK:(c + 1) * FF_CHUNK] = (g * jax.nn.sigmoid(g) * u).astype(BF16)
        if pending and c % INTERLEAVE_EVERY == 1:
            pending.pop(0)()
    assert not pending


def _swiglu_down(x_rows, a_rows, o_rows, gate, wo_ref, g_ref, b_ref, piece):
    y = ALPHA * x_rows[piece, :] + (0.5 * (1.0 + gate)) * _dot(a_rows[piece, :], wo_ref[...])
    o_rows[piece, :] = _layer_norm(y, g_ref[...], b_ref[...])


def _run_sub_tiles(nsub, prepare, up, finish):
    pieces = _ln_pieces()
    for piece in pieces:
        prepare(0, piece)
    for s in range(nsub):
        tasks = [functools.partial(finish, s - 1, piece) for piece in pieces] if s > 0 else []
        tasks += [functools.partial(prepare, s + 1, piece) for piece in pieces] if s + 1 < nsub else []
        up(s, tasks)
    for piece in pieces:
        finish(nsub - 1, piece)


def _ffn_kernel(x_ref, mod_ref, wi_ref, wo_ref, g_ref, b_ref, o_ref, h_scr, a_scr):
    m = mod_ref[0]

    def prepare(s, piece):
        h_scr[s, piece, :] = _modulate_rows(x_ref[0, pl.ds(s * FFN_SUB + piece.start, LN_ROWS), :], m, 0)

    def up(s, tasks):
        _swiglu_up(h_scr.at[s], a_scr.at[s], wi_ref, tasks)

    def finish(s, piece):
        rows = pl.ds(s * FFN_SUB, FFN_SUB)
        _swiglu_down(x_ref.at[0, rows], a_scr.at[s], o_ref.at[0, rows], m[2:3], wo_ref, g_ref, b_ref, piece)

    _run_sub_tiles(h_scr.shape[0], prepare, up, finish)


def _mix_ffn_kernel(x_ref, oa_ref, oh_ref, mod_ref, w_out_ref, gn_ref, gm_ref, bm_ref, wi_ref, wo_ref, g_ref, b_ref,
                    o_ref, h_scr, a_scr, x_scr):
    m = mod_ref[0]
    gn = gn_ref[...]

    def prepare(s, piece):
        rows = pl.ds(s * FFN_SUB + piece.start, LN_ROWS)
        a = _rms_norm(oa_ref[0, rows, :].astype(F32), gn[:, :D_ATTN]).astype(BF16)
        h = _rms_norm(oh_ref[0, rows, :].astype(F32), gn[:, D_ATTN:]).astype(BF16)
        o = _dot(a, w_out_ref[:D_ATTN, :]) + _dot(h, w_out_ref[D_ATTN:, :])
        x = _layer_norm(ALPHA * x_ref[0, rows, :] + (1.0 + m[5:6]) * o, gm_ref[...], bm_ref[...])
        x_scr[s, piece, :] = x
        h_scr[s, piece, :] = _modulate_rows(x, m, 6)

    def up(s, tasks):
        _swiglu_up(h_scr.at[s], a_scr.at[s], wi_ref, tasks)

    def finish(s, piece):
        _swiglu_down(x_scr.at[s], a_scr.at[s], o_ref.at[0, pl.ds(s * FFN_SUB, FFN_SUB)], m[8:9],
                     wo_ref, g_ref, b_ref, piece)

    _run_sub_tiles(h_scr.shape[0], prepare, up, finish)


def _ffn(x, mod, wi_c, wo, ln_g, ln_b, mix=None):
    b, l, _ = x.shape
    tm = min(FFN_TILE, l)
    nsub = tm // FFN_SUB
    nck = D_FF // FF_CHUNK
    row = lambda i, j: (i, j, 0)
    vec = _resident((1, D_MODEL))
    x_spec = pl.BlockSpec((1, tm, D_MODEL), row)
    mod_spec = pl.BlockSpec((1, N_MOD, D_MODEL), lambda i, j: (i, 0, 0))
    ffn_specs = [_resident((nck, D_MODEL, 2 * FF_CHUNK)), _resident((D_FF, D_MODEL)), vec, vec]
    scratch = [pltpu.VMEM((nsub, FFN_SUB, D_MODEL), BF16), pltpu.VMEM((nsub, FFN_SUB, D_FF), BF16)]
    if mix is None:
        body, name = _ffn_kernel, "ffn"
        in_specs = [x_spec, mod_spec] + ffn_specs
        args = (x, mod, wi_c, wo, ln_g, ln_b)
    else:
        body, name = _mix_ffn_kernel, "mix_ffn"
        half = pl.BlockSpec((1, tm, D_ATTN), row)
        in_specs = [x_spec, half, half, mod_spec, _resident((D_MODEL, D_MODEL)), vec, vec, vec] + ffn_specs
        args = (x, mix[0], mix[1], mod, *mix[2:], wi_c, wo, ln_g, ln_b)
        scratch.append(pltpu.VMEM((nsub, FFN_SUB, D_MODEL), F32))
    return pl.pallas_call(
        body,
        out_shape=jax.ShapeDtypeStruct(x.shape, F32),
        grid=(b, l // tm),
        in_specs=in_specs,
        out_specs=x_spec,
        scratch_shapes=scratch,
        compiler_params=_params(("arbitrary", "arbitrary"), 60),
        name=name,
    )(*args)


def _rope_slab(t, cos, sin_up, sin_dn):
    return (t * cos + pltpu.roll(t, ROT_DIM // 2, 1) * sin_up
            + pltpu.roll(t, LANES - ROT_DIM // 2, 1) * sin_dn)


def _in_proj_kernel(x_ref, xp_ref, xn_ref, mod_ref, w_ref, cw_ref, cb_ref, cos_ref, sup_ref, sdn_ref,
                    q_ref, kd_ref, vd_ref, p_ref, x0_ref):
    j = pl.program_id(1)
    tm = x_ref.shape[1]
    m = mod_ref[0]
    x = jnp.concatenate([xp_ref[0], x_ref[0], xn_ref[0]], axis=0)
    h = (x * (1.0 + m[4:5]) + m[3:4]).astype(BF16)
    inner = slice(HALO, HALO + tm)
    first_tile, last_tile = j == 0, j == pl.num_programs(1) - 1

    def conv(col):
        lanes = slice(col, col + CONV_LANES)
        z = _dot(h, w_ref[:, D_ATTN + 2 * KV_WIDTH + col:D_ATTN + 2 * KV_WIDTH + col + CONV_LANES])
        z = jnp.concatenate([jnp.where(first_tile, 0.0, z[:HALO]), z[inner],
                             jnp.where(last_tile, 0.0, z[HALO + tm:])], axis=0)
        return (pltpu.roll(z, 1, 0)[inner] * cw_ref[0:1, lanes] + z[inner] * cw_ref[1:2, lanes]
                + pltpu.roll(z, tm + 2 * HALO - 1, 0)[inner] * cw_ref[2:3, lanes] + cb_ref[:, lanes])

    z = _dot(h[inner], w_ref[:, :D_ATTN + 2 * KV_WIDTH])
    cos, sup, sdn = cos_ref[...], sup_ref[...], sdn_ref[...]
    for s in range(D_ATTN // LANES):
        q = _rope_slab(z[:, s * LANES:(s + 1) * LANES], cos, sup, sdn)
        q_ref[0, :, s * LANES:(s + 1) * LANES] = (q * Q_SCALE).astype(BF16)
    k = _rope_slab(z[:, D_ATTN:D_ATTN + KV_WIDTH], cos, sup, sdn)
    v = z[:, D_ATTN + KV_WIDTH:D_ATTN + 2 * KV_WIDTH]
    lane = lax.broadcasted_iota(jnp.int32, k.shape, 1)
    first = lane < HEAD_DIM
    for t, ref in ((k, kd_ref), (v, vd_ref)):
        swapped = pltpu.roll(t, HEAD_DIM, 1)
        slabs = (jnp.where(first, t, 0.0), jnp.where(first, 0.0, swapped),
                 jnp.where(first, swapped, 0.0), jnp.where(first, 0.0, t))
        for i, slab in enumerate(slabs):
            ref[0, :, i * LANES:(i + 1) * LANES] = slab.astype(BF16)
    for t in range(D_HY // CONV_LANES):
        lanes = slice(t * CONV_LANES, (t + 1) * CONV_LANES)
        x0, x1, hv = (conv(g * D_HY + t * CONV_LANES) for g in range(3))
        p_ref[0, :, lanes] = (hv * x1).astype(BF16)
        x0_ref[0, :, lanes] = x0.astype(BF16)


def _in_proj(x, mod, w_in, conv_w, conv_b, rope):
    b, l, _ = x.shape
    tm = min(ROW_TILE, l)
    per = tm // HALO
    last = l // HALO - 1
    row = lambda i, j: (i, j, 0)
    tab = pl.BlockSpec((tm, LANES), lambda i, j: (j, 0))
    return pl.pallas_call(
        _in_proj_kernel,
        out_shape=(
            jax.ShapeDtypeStruct((b, l, D_ATTN), BF16),
            jax.ShapeDtypeStruct((b, l, KV_SLABS), BF16),
            jax.ShapeDtypeStruct((b, l, KV_SLABS), BF16),
            jax.ShapeDtypeStruct((b, l, D_HY), BF16),
            jax.ShapeDtypeStruct((b, l, D_HY), BF16),
        ),
        grid=(b, l // tm),
        in_specs=[
            pl.BlockSpec((1, tm, D_MODEL), row),
            pl.BlockSpec((1, HALO, D_MODEL), lambda i, j: (i, jnp.maximum(j * per - 1, 0), 0)),
            pl.BlockSpec((1, HALO, D_MODEL), lambda i, j: (i, jnp.minimum((j + 1) * per, last), 0)),
            pl.BlockSpec((1, N_MOD, D_MODEL), lambda i, j: (i, 0, 0)),
            _resident((D_MODEL, N_IN)),
            _resident((3, 3 * D_HY)),
            _resident((1, 3 * D_HY)),
            tab, tab, tab,
        ],
        out_specs=(
            pl.BlockSpec((1, tm, D_ATTN), row),
            pl.BlockSpec((1, tm, KV_SLABS), row),
            pl.BlockSpec((1, tm, KV_SLABS), row),
            pl.BlockSpec((1, tm, D_HY), row),
            pl.BlockSpec((1, tm, D_HY), row),
        ),
        compiler_params=_params(("arbitrary", "arbitrary"), 48),
        name="in_proj",
    )(x, x, x, mod, w_in, conv_w, conv_b.reshape(1, -1), *rope)


def _rope_tables(l):
    half = ROT_DIM // 2
    inv = ROPE_THETA ** (-jnp.arange(0, ROT_DIM, 2, dtype=F32) / ROT_DIM)
    ang = jnp.arange(l, dtype=F32)[:, None] * inv[None]
    cos, sin = jnp.cos(ang), jnp.sin(ang)
    rest = HEAD_DIM - ROT_DIM
    cos_h = jnp.concatenate([cos, cos, jnp.ones((l, rest), F32)], axis=1)
    up_h = jnp.concatenate([jnp.zeros((l, half), F32), sin, jnp.zeros((l, rest), F32)], axis=1)
    dn_h = jnp.concatenate([-sin, jnp.zeros((l, half + rest), F32)], axis=1)
    reps = LANES // HEAD_DIM
    return tuple(jnp.tile(t, (1, reps)) for t in (cos_h, up_h, dn_h))


def _attn_kernel(sink_ref, q_ref, kd_ref, vd_ref, o_ref, *, seq):
    j = pl.program_id(1)
    first = lax.broadcasted_iota(jnp.int32, (BAND, LANES), 1) < HEAD_DIM
    out_first = lax.broadcasted_iota(jnp.int32, (Q_BLOCK, LANES), 1) < HEAD_DIM
    ones_ext = jnp.concatenate([jnp.where(first, 1.0, 0.0), jnp.where(first, 0.0, 1.0)], axis=0).astype(BF16)
    for blk in range(q_ref.shape[1] // Q_BLOCK):
        q0 = j * q_ref.shape[1] + blk * Q_BLOCK
        start = pl.multiple_of(jnp.clip(q0 - Q_BLOCK, 0, seq - BAND), Q_BLOCK)
        qpos = q0 + lax.broadcasted_iota(jnp.int32, (Q_BLOCK, BAND), 0)
        kpos = start + lax.broadcasted_iota(jnp.int32, (Q_BLOCK, BAND), 1)
        bias = jnp.where(jnp.abs(qpos - kpos) <= WINDOW, 0.0, NEG_BIG).astype(F32)
        rows = slice(blk * Q_BLOCK, (blk + 1) * Q_BLOCK)
        for kh in range(N_KV_HEADS):
            kb, vb = (jnp.concatenate([ref[0, pl.ds(start, BAND), (2 * kh + e) * LANES:(2 * kh + e + 1) * LANES]
                                       for e in range(2)], axis=0) for ref in (kd_ref, vd_ref))
            vb_ext = jnp.concatenate([vb, ones_ext], axis=1)
            q = jnp.concatenate([q_ref[0, rows, (2 * kh + r) * LANES:(2 * kh + r + 1) * LANES] for r in range(2)],
                                axis=0)
            s = lax.dot_general(q, kb, (((1,), (1,)), ((), ())), preferred_element_type=F32)
            probs, sink_terms = [], []
            for r in range(2):
                pr, sk = [], []
                for e in range(2):
                    sink = sink_ref[4 * kh + 2 * r + e] * LOG2E
                    se = s[r * Q_BLOCK:(r + 1) * Q_BLOCK, e * BAND:(e + 1) * BAND] + bias
                    mx = jnp.maximum(jnp.max(se, axis=1, keepdims=True), sink)
                    pr.append(jnp.exp2(se - mx).astype(BF16))
                    sk.append(jnp.exp2(sink - mx))
                probs.append(jnp.concatenate(pr, axis=1))
                sink_terms.append(jnp.where(out_first, sk[0], sk[1]))
            o = _dot(jnp.concatenate(probs, axis=0), vb_ext)
            for r in range(2):
                part = o[r * Q_BLOCK:(r + 1) * Q_BLOCK]
                den = part[:, LANES:] + sink_terms[r]
                o_ref[0, rows, (2 * kh + r) * LANES:(2 * kh + r + 1) * LANES] = (
                    part[:, :LANES] / den).astype(o_ref.dtype)


def _attention(q, kd, vd, sink):
    b, l, _ = q.shape
    tq = min(Q_TILE, l)
    return pl.pallas_call(
        functools.partial(_attn_kernel, seq=l),
        out_shape=jax.ShapeDtypeStruct((b, l, D_ATTN), BF16),
        grid=(b, l // tq),
        in_specs=[
            pl.BlockSpec(memory_space=pltpu.SMEM),
            pl.BlockSpec((1, tq, D_ATTN), lambda i, j: (i, j, 0)),
            pl.BlockSpec((1, l, KV_SLABS), lambda i, j: (i, 0, 0)),
            pl.BlockSpec((1, l, KV_SLABS), lambda i, j: (i, 0, 0)),
        ],
        out_specs=pl.BlockSpec((1, tq, D_ATTN), lambda i, j: (i, j, 0)),
        compiler_params=_params(("arbitrary", "arbitrary"), 32),
        name="window_attn",
    )(sink, q, kd, vd)


def _filter_kernel(z_ref, w1_ref, b1_ref, w2_ref, b2_ref, w3_ref, b3_ref, w4_ref, fr_ref, dec_ref,
                   e_ref, d_ref):
    z = z_ref[...]
    fr = fr_ref[...]
    h = jnp.sin(fr * (_dot3(z, w1_ref[...]) + b1_ref[...]))
    h = jnp.sin(fr * (_dot3(h, w2_ref[...]) + b2_ref[...]))
    h = jnp.sin(fr * (_dot3(h, w3_ref[...]) + b3_ref[...]))
    filt = _dot3(h, w4_ref[...]) * jnp.exp(-z[:, 0:1] * jnp.abs(dec_ref[...]))
    fwd, bwd = filt[:, :D_HY], filt[:, D_HY:]
    e_ref[...] = (fwd + bwd).astype(BF16)
    d_ref[...] = (fwd - bwd).astype(BF16)


def _filter_features(l):
    t = jnp.linspace(0.0, 1.0, l, dtype=F32)[:, None]
    n_bands = (FILTER_EMB - 1) // 2
    w = 2.0 * math.pi * jnp.arange(l, dtype=F32)[:, None] / l
    fb = jnp.linspace(1e-4, n_bands - 1, n_bands, dtype=F32)[None]
    z = jnp.concatenate([t, jnp.cos(fb * w), -jnp.sin(fb * w)], axis=-1)
    return jnp.pad(z, ((0, 0), (0, FILTER_HID - FILTER_EMB)))


def _filters(z, w1, b1, w2, b2, w3, b3, w4, freq, decay):
    l = z.shape[0]
    tl = min(ROW_TILE, l)
    w1p = jnp.pad(w1, ((0, FILTER_HID - FILTER_EMB), (0, 0)))
    vec = lambda a: a.reshape(1, -1)
    out = jax.ShapeDtypeStruct((l, D_HY), BF16)
    tile = pl.BlockSpec((tl, D_HY), lambda i: (i, 0))
    return pl.pallas_call(
        _filter_kernel,
        out_shape=(out, out),
        grid=(l // tl,),
        in_specs=[
            pl.BlockSpec((tl, FILTER_HID), lambda i: (i, 0)),
            _resident((FILTER_HID, FILTER_HID)), _resident((1, FILTER_HID)),
            _resident((FILTER_HID, FILTER_HID)), _resident((1, FILTER_HID)),
            _resident((FILTER_HID, FILTER_HID)), _resident((1, FILTER_HID)),
            _resident((FILTER_HID, 2 * D_HY)), _resident((1, FILTER_HID)), _resident((1, 2 * D_HY)),
        ],
        out_specs=(tile, tile),
        compiler_params=_params(("arbitrary",), 32),
        name="hyena_filter",
    )(z, w1p, vec(b1), w2, vec(b2), w3, vec(b3), w4, vec(freq), vec(decay))


def _fft_matrices(l):
    n = 2 * l
    n2 = FFT_N2
    n1 = n // n2
    kk = n1 // 2 + 1
    kh = jnp.arange(kk, dtype=jnp.int32)
    t1 = jnp.arange(n1 // 2, dtype=jnp.int32)
    ang1 = ((kh[:, None] * t1[None, :]) % n1).astype(F32) * (2.0 * math.pi / n1)
    f1 = jnp.stack([jnp.cos(ang1), jnp.sin(ang1)], axis=1).reshape(2 * kk, n1 // 2)
    eye = jnp.eye(ROW_GROUP, dtype=F32)
    kron = (f1[:, None, :, None] * eye[None, :, None, :]).reshape(2 * kk * ROW_GROUP, (n1 // 2) * ROW_GROUP)
    k1 = jnp.arange(n1, dtype=jnp.int32)
    k2 = jnp.arange(n2 // 2, dtype=jnp.int32)
    t2 = jnp.arange(n2, dtype=jnp.int32)
    ang_a = ((k1[:, None] * t2[None, :]) % n).astype(F32) * (2.0 * math.pi / n)
    ang_b = ((k2[:, None] * t2[None, :]) % n2).astype(F32) * (2.0 * math.pi / n2)
    ca, sa = jnp.cos(ang_a)[:, None, :], jnp.sin(ang_a)[:, None, :]
    cb, sb = jnp.cos(ang_b)[None], jnp.sin(ang_b)[None]
    c, s = ca * cb - sa * sb, sa * cb + ca * sb
    re_rows = jnp.concatenate([c, -s], axis=2)
    im_rows = jnp.concatenate([s, c], axis=2)
    nyq = jnp.concatenate([jnp.where(t2 % 2 == 0, 1.0, -1.0).astype(F32), jnp.zeros((n2,), F32)])
    im_rows = im_rows.at[0, 0, :].set(nyq)
    conj = jnp.concatenate([jnp.ones((n2,), F32), -jnp.ones((n2,), F32)])
    mate = (n1 - kh) % n1
    has_mate = ((kh > 0) & (kh < n1 // 2))[:, None, None]
    re_mate = jnp.where(has_mate, re_rows[mate] * conj, 0.0)
    im_mate = jnp.where(has_mate, im_rows[mate] * conj, 0.0)
    g = jnp.concatenate([re_rows[:kk], re_mate, im_rows[:kk], im_mate], axis=1)
    return dict(kron=kron.astype(BF16), kron_t=kron.T.astype(BF16),
                g=g.astype(BF16), g_t=g.transpose(0, 2, 1).astype(BF16))


def _fft_stage1(x_ref, kron_ref, b_scr):
    n1h, n2, c = x_ref.shape

    def group(g, carry):
        grp = _row_group(g)
        x = x_ref[:, grp, :].reshape(n1h * ROW_GROUP, c)
        b_scr[:, grp, :] = _dot(kron_ref[...], x).astype(BF16).reshape(b_scr.shape[0], ROW_GROUP, c)
        return carry

    lax.fori_loop(0, n2 // ROW_GROUP, group, 0, unroll=FFT_UNROLL)


def _row_group(g):
    return pl.ds(pl.multiple_of(g * ROW_GROUP, ROW_GROUP), ROW_GROUP)


def _fft_stage2(g_ref, b_scr, k):
    n2, c = b_scr.shape[1:]
    return _dot(g_ref[k], b_scr[pl.ds(2 * k, 2)].reshape(2 * n2, c))


def _spectrum_kernel(e_ref, d_ref, kron_ref, g_ref, ka_ref, kb_ref, ka2_ref, be_scr, bd_scr, *, seq):
    half = ka_ref.shape[1]
    _fft_stage1(e_ref, kron_ref, be_scr)
    _fft_stage1(d_ref, kron_ref, bd_scr)
    first_row = lax.broadcasted_iota(jnp.int32, (half, 1), 0) == 0

    def freq(k, carry):
        n2, c = be_scr.shape[1:]
        be = be_scr[pl.ds(2 * k, 2)].reshape(2 * n2, c)
        bd = bd_scr[pl.ds(2 * k, 2)].reshape(2 * n2, c)
        re = _dot(g_ref[k, :half, :], be)
        im = _dot(g_ref[k, half:, :], bd)
        nyq = _dot(g_ref[k, half:half + ROW_GROUP, :], be)[0:1]
        is_dc = first_row & (k == 0)
        ka = re * jnp.where(is_dc, 0.5 / seq, 1.0 / seq)
        ka_ref[k] = ka
        kb_ref[k] = jnp.where(is_dc, 0.0, im * (1.0 / seq))
        ka2_ref[k] = jnp.where(is_dc, nyq * (0.5 / seq), ka)
        return carry

    lax.fori_loop(0, ka_ref.shape[0], freq, 0, unroll=FFT_UNROLL)


def _spectrum(mats, e, d):
    l = e.shape[0]
    kk, n2 = mats["g"].shape[0], FFT_N2
    n1h = l // n2
    cw = LANES
    out = jax.ShapeDtypeStruct((kk, n2, D_HY), F32)
    src = pl.BlockSpec((n1h, n2, cw), lambda h: (0, 0, h))
    dst = pl.BlockSpec((kk, n2, cw), lambda h: (0, 0, h))
    view = lambda a: a.reshape(n1h, n2, D_HY)
    return pl.pallas_call(
        functools.partial(_spectrum_kernel, seq=l),
        out_shape=(out, out, out),
        grid=(D_HY // cw,),
        in_specs=[src, src, _resident(mats["kron"].shape), _resident(mats["g"].shape)],
        out_specs=(dst, dst, dst),
        scratch_shapes=[pltpu.VMEM((2 * kk, n2, cw), BF16), pltpu.VMEM((2 * kk, n2, cw), BF16)],
        compiler_params=_params(("arbitrary",), 56),
        name="hyena_spectrum",
    )(view(e), view(d), mats["kron"], mats["g"])


def _long_conv_kernel(p_ref, x0_ref, kron_ref, kron_t_ref, g_ref, gt_ref, ka_ref, kb_ref, ka2_ref, bias_ref,
                      o_ref, b_scr):
    n1h, n2, c = p_ref.shape[1:]
    half = g_ref.shape[1] // 2
    _fft_stage1(p_ref.at[0], kron_ref, b_scr)

    def freq(k, carry):
        u = _fft_stage2(g_ref, b_scr, k)
        ure, uim = u[:half], u[half:]
        ka, kb, ka2 = ka_ref[k], kb_ref[k], ka2_ref[k]
        y = jnp.concatenate([ure * ka - uim * kb, ure * kb + uim * ka2], axis=0).astype(BF16)
        b_scr[pl.ds(2 * k, 2)] = _dot(gt_ref[k], y).astype(BF16).reshape(2, n2, c)
        return carry

    lax.fori_loop(0, g_ref.shape[0], freq, 0, unroll=FFT_UNROLL)
    bias = bias_ref[...]

    def group(g, carry):
        grp = _row_group(g)
        z = b_scr[:, grp, :].reshape(b_scr.shape[0] * ROW_GROUP, c)
        y = _dot(kron_t_ref[...], z).reshape(n1h, ROW_GROUP, c)
        gated = (y + p_ref[0, :, grp, :].astype(F32) * bias) * x0_ref[0, :, grp, :].astype(F32)
        o_ref[0, :, grp, :] = gated.astype(BF16)
        return carry

    lax.fori_loop(0, n2 // ROW_GROUP, group, 0, unroll=FFT_UNROLL)


def _long_conv(p, x0, mats, ka, kb, ka2, bias):
    b, l, _ = p.shape
    kk, n2 = mats["g"].shape[0], FFT_N2
    n1h = l // n2
    cw = FFT_LANES
    view = lambda a: a.reshape(b, n1h, n2, D_HY)
    tile = pl.BlockSpec((1, n1h, n2, cw), lambda h, i: (i, 0, 0, h))
    spec = pl.BlockSpec((kk, n2, cw), lambda h, i: (0, 0, h), pipeline_mode=pl.Buffered(1))
    out = pl.pallas_call(
        _long_conv_kernel,
        out_shape=jax.ShapeDtypeStruct((b, n1h, n2, D_HY), BF16),
        grid=(D_HY // cw, b),
        in_specs=[
            tile, tile,
            _resident(mats["kron"].shape), _resident(mats["kron_t"].shape),
            _resident(mats["g"].shape), _resident(mats["g_t"].shape),
            spec, spec, spec,
            pl.BlockSpec((1, cw), lambda h, i: (0, h)),
        ],
        out_specs=tile,
        scratch_shapes=[pltpu.VMEM((2 * kk, n2, cw), BF16)],
        compiler_params=_params(("arbitrary", "arbitrary"), 60),
        name="hyena_long_conv",
    )(view(p), view(x0), mats["kron"], mats["kron_t"], mats["g"], mats["g_t"], ka, kb, ka2, bias.reshape(1, -1))
    return out.reshape(b, l, D_HY)


def _chunk_ffn_weights(wi, wo):
    nck = D_FF // FF_CHUNK
    g = wi[:, :D_FF].reshape(D_MODEL, nck, FF_CHUNK)
    u = wi[:, D_FF:].reshape(D_MODEL, nck, FF_CHUNK)
    wi_c = jnp.concatenate([g, u], axis=2).transpose(1, 0, 2).astype(BF16)
    return wi_c, wo.astype(BF16)


def _trunk(x, mod, weights, p):
    l = x.shape[1]
    depth = mod.shape[0]
    rope = _rope_tables(l)
    feats = _filter_features(l)
    mats = _fft_matrices(l)
    vec = lambda a: a.reshape(1, -1)
    for i in range(depth):
        w = weights[i]
        m = mod[i]
        x = _ffn(x, m, w["wi1"], w["wo1"], vec(p["ln_g"][i, 0]), vec(p["ln_b"][i, 0]))
        q, kd, vd, gated, x0 = _in_proj(x, m, w["w_in"], p["hy_conv_w"][i], p["hy_conv_b"][i], rope)
        o_attn = _attention(q, kd, vd, p["sink"][i])
        e, d = _filters(feats, p["hy_w1"][i], p["hy_b1"][i], p["hy_w2"][i], p["hy_b2"][i],
                        p["hy_w3"][i], p["hy_b3"][i], p["hy_w4"][i], p["hy_freq"][i], p["hy_decay"][i])
        ka, kb, ka2 = _spectrum(mats, e, d)
        o_hy = _long_conv(gated, x0, mats, ka, kb, ka2, p["hy_bias"][i])
        mix = (o_attn, o_hy, w["w_out"], vec(p["grp_norm_g"][i]), vec(p["ln_g"][i, 1]), vec(p["ln_b"][i, 1]))
        x = _ffn(x, m, w["wi2"], w["wo2"], vec(p["ln_g"][i, 2]), vec(p["ln_b"][i, 2]), mix)
    return x


def kernel(x_prompt, x_sample, c_prompt, c_sample, ada_w, ada_b, ffn1_wi, ffn1_wo, ffn2_wi, ffn2_wo, ln_g, ln_b, w_in, w_out, sink, grp_norm_g, hy_conv_w, hy_conv_b, hy_w1, hy_b1, hy_w2, hy_b2, hy_w3, hy_b3, hy_w4, hy_freq, hy_decay, hy_bias):
    p = dict(ln_g=ln_g, ln_b=ln_b, sink=sink, grp_norm_g=grp_norm_g, hy_conv_w=hy_conv_w,
             hy_conv_b=hy_conv_b, hy_w1=hy_w1, hy_b1=hy_b1, hy_w2=hy_w2, hy_b2=hy_b2, hy_w3=hy_w3,
             hy_b3=hy_b3, hy_w4=hy_w4, hy_freq=hy_freq, hy_decay=hy_decay, hy_bias=hy_bias)
    depth = ada_w.shape[0]
    weights = []
    for i in range(depth):
        wi1, wo1 = _chunk_ffn_weights(ffn1_wi[i], ffn1_wo[i])
        wi2, wo2 = _chunk_ffn_weights(ffn2_wi[i], ffn2_wo[i])
        weights.append(dict(wi1=wi1, wo1=wo1, wi2=wi2, wo2=wo2,
                            w_in=w_in[i].astype(BF16), w_out=w_out[i].astype(BF16)))
    nb = c_prompt.shape[0]
    mod = _modulation(jnp.concatenate([c_prompt, c_sample], axis=0), ada_w, ada_b)
    y_prompt = _trunk(x_prompt, mod[:, :nb], weights, p)
    y_sample = _trunk(x_sample, mod[:, nb:], weights, p)
    return (y_prompt, y_sample)
```

```python
import functools
import math

import jax
import jax.numpy as jnp
from jax import lax
from jax.experimental import pallas as pl
from jax.experimental.pallas import tpu as pltpu

F32 = jnp.float32
BF16 = jnp.bfloat16

D_MODEL = 1024
D_ATTN = 512
D_HY = 512
HEAD_DIM = 64
N_Q_HEADS = D_ATTN // HEAD_DIM
N_KV_HEADS = 2
KV_WIDTH = N_KV_HEADS * HEAD_DIM
ROT_DIM = HEAD_DIM // 4
ROPE_THETA = 500000.0
WINDOW = 128
FILTER_EMB = 33
FILTER_HID = 64
D_FF = 2816
N_IN = D_ATTN + 2 * KV_WIDTH + 3 * D_HY
N_MOD = 9
DEPTH = 4
ALPHA = float((2 * DEPTH) ** 0.25)
LN_EPS = 1e-5
RMS_EPS = 1e-6
NEG_BIG = -1e30
LOG2E = math.log2(math.e)
Q_SCALE = HEAD_DIM ** -0.5 * LOG2E

LANES = 128
KV_SLABS = 2 * N_KV_HEADS * LANES
VMEM_BYTES = 64 * 1024 * 1024
MIB = 1024 * 1024

ROW_TILE = 1024
FFN_TILE = 1024
FFN_SUB = 512
FF_CHUNK = 256
LN_ROWS = 256
INTERLEAVE_EVERY = 2
Q_TILE = 2048
Q_BLOCK = 128
BAND = 3 * Q_BLOCK
FFT_N2 = 256
FFT_LANES = 256
ROW_GROUP = 16
FFT_UNROLL = 8
MOD_N_TILE = 1536
HALO = 16
CONV_LANES = 256


def _params(semantics, vmem_mib):
    return pltpu.CompilerParams(dimension_semantics=semantics,
                                vmem_limit_bytes=min(vmem_mib * MIB, VMEM_BYTES - 4 * MIB))


def _resident(shape):
    nd = len(shape)
    return pl.BlockSpec(shape, lambda *_: (0,) * nd, pipeline_mode=pl.Buffered(1))


def _dot(a, b):
    return jnp.dot(a, b, preferred_element_type=F32)


def _split(a):
    hi = a.astype(BF16)
    lo = (a - hi.astype(F32)).astype(BF16)
    return hi, lo


def _dot3(a, b):
    a_hi, a_lo = _split(a)
    b_hi, b_lo = _split(b)
    return _dot(a_hi, b_hi) + _dot(a_hi, b_lo) + _dot(a_lo, b_hi)


def _layer_norm(y, g, b):
    mu = jnp.mean(y, axis=-1, keepdims=True)
    d = y - mu
    var = jnp.mean(d * d, axis=-1, keepdims=True)
    return d * lax.rsqrt(var + LN_EPS) * g + b


def _rms_norm(y, g):
    ms = jnp.mean(y * y, axis=-1, keepdims=True)
    return y * lax.rsqrt(ms + RMS_EPS) * g


def _mod_kernel(c_ref, w_ref, b_ref, o_ref):
    c = c_ref[...]
    a = c * jax.nn.sigmoid(c)
    o_ref[0] = _dot3(a, w_ref[0]) + b_ref[0]


def _modulation(c, ada_w, ada_b):
    depth = ada_w.shape[0]
    b = c.shape[0]
    n = N_MOD * D_MODEL
    out = pl.pallas_call(
        _mod_kernel,
        out_shape=jax.ShapeDtypeStruct((depth, b, n), F32),
        grid=(depth, n // MOD_N_TILE),
        in_specs=[
            pl.BlockSpec((b, D_MODEL), lambda l, j: (0, 0)),
            pl.BlockSpec((1, D_MODEL, MOD_N_TILE), lambda l, j: (l, 0, j)),
            pl.BlockSpec((1, 1, MOD_N_TILE), lambda l, j: (l, 0, j)),
        ],
        out_specs=pl.BlockSpec((1, b, MOD_N_TILE), lambda l, j: (l, 0, j)),
        compiler_params=_params(("arbitrary", "arbitrary"), 40),
        name="adaln_mod",
    )(c, ada_w, ada_b.reshape(depth, 1, n))
    return out.reshape(depth, b, N_MOD, D_MODEL)


def _ln_pieces():
    return [slice(r * LN_ROWS, (r + 1) * LN_ROWS) for r in range(FFN_SUB // LN_ROWS)]


def _modulate_rows(x, m, mod_row):
    return (x * (1.0 + m[mod_row + 1:mod_row + 2]) + m[mod_row:mod_row + 1]).astype(BF16)


def _swiglu_up(h_rows, a_rows, wi_ref, interleave):
    pending = list(interleave)
    for c in range(D_FF // FF_CHUNK):
        gu = _dot(h_rows[...], wi_ref[c])
        g, u = gu[:, :FF_CHUNK], gu[:, FF_CHUNK:]
        a_rows[:, c * FF_CHUNK:(c + 1) * FF_CHUNK] = (g * jax.nn.sigmoid(g) * u).astype(BF16)
        if pending and c % INTERLEAVE_EVERY == 1:
            pending.pop(0)()
    assert not pending


def _swiglu_down(x_rows, a_rows, o_rows, gate, wo_ref, g_ref, b_ref, piece):
    y = ALPHA * x_rows[piece, :] + (0.5 * (1.0 + gate)) * _dot(a_rows[piece, :], wo_ref[...])
    o_rows[piece, :] = _layer_norm(y, g_ref[...], b_ref[...])


def _run_sub_tiles(nsub, prepare, up, finish):
    pieces = _ln_pieces()
    for piece in pieces:
        prepare(0, piece)
    for s in range(nsub):
        tasks = [functools.partial(finish, s - 1, piece) for piece in pieces] if s > 0 else []
        tasks += [functools.partial(prepare, s + 1, piece) for piece in pieces] if s + 1 < nsub else []
        up(s, tasks)
    for piece in pieces:
        finish(nsub - 1, piece)


def _ffn_kernel(x_ref, mod_ref, wi_ref, wo_ref, g_ref, b_ref, o_ref, h_scr, a_scr):
    m = mod_ref[0]

    def prepare(s, piece):
        h_scr[s, piece, :] = _modulate_rows(x_ref[0, pl.ds(s * FFN_SUB + piece.start, LN_ROWS), :], m, 0)

    def up(s, tasks):
        _swiglu_up(h_scr.at[s], a_scr.at[s], wi_ref, tasks)

    def finish(s, piece):
        rows = pl.ds(s * FFN_SUB, FFN_SUB)
        _swiglu_down(x_ref.at[0, rows], a_scr.at[s], o_ref.at[0, rows], m[2:3], wo_ref, g_ref, b_ref, piece)

    _run_sub_tiles(h_scr.shape[0], prepare, up, finish)


def _mix_ffn_kernel(x_ref, oa_ref, oh_ref, mod_ref, w_out_ref, gn_ref, gm_ref, bm_ref, wi_ref, wo_ref, g_ref, b_ref,
                    o_ref, h_scr, a_scr, x_scr):
    m = mod_ref[0]
    gn = gn_ref[...]

    def prepare(s, piece):
        rows = pl.ds(s * FFN_SUB + piece.start, LN_ROWS)
        a = _rms_norm(oa_ref[0, rows, :].astype(F32), gn[:, :D_ATTN]).astype(BF16)
        h = _rms_norm(oh_ref[0, rows, :].astype(F32), gn[:, D_ATTN:]).astype(BF16)
        o = _dot(a, w_out_ref[:D_ATTN, :]) + _dot(h, w_out_ref[D_ATTN:, :])
        x = _layer_norm(ALPHA * x_ref[0, rows, :] + (1.0 + m[5:6]) * o, gm_ref[...], bm_ref[...])
        x_scr[s, piece, :] = x
        h_scr[s, piece, :] = _modulate_rows(x, m, 6)

    def up(s, tasks):
        _swiglu_up(h_scr.at[s], a_scr.at[s], wi_ref, tasks)

    def finish(s, piece):
        _swiglu_down(x_scr.at[s], a_scr.at[s], o_ref.at[0, pl.ds(s * FFN_SUB, FFN_SUB)], m[8:9],
                     wo_ref, g_ref, b_ref, piece)

    _run_sub_tiles(h_scr.shape[0], prepare, up, finish)


def _ffn(x, mod, wi_c, wo, ln_g, ln_b, mix=None):
    b, l, _ = x.shape
    tm = min(FFN_TILE, l)
    nsub = tm // FFN_SUB
    nck = D_FF // FF_CHUNK
    row = lambda i, j: (i, j, 0)
    vec = _resident((1, D_MODEL))
    x_spec = pl.BlockSpec((1, tm, D_MODEL), row)
    mod_spec = pl.BlockSpec((1, N_MOD, D_MODEL), lambda i, j: (i, 0, 0))
    ffn_specs = [_resident((nck, D_MODEL, 2 * FF_CHUNK)), _resident((D_FF, D_MODEL)), vec, vec]
    scratch = [pltpu.VMEM((nsub, FFN_SUB, D_MODEL), BF16), pltpu.VMEM((nsub, FFN_SUB, D_FF), BF16)]
    if mix is None:
        body, name = _ffn_kernel, "ffn"
        in_specs = [x_spec, mod_spec] + ffn_specs
        args = (x, mod, wi_c, wo, ln_g, ln_b)
    else:
        body, name = _mix_ffn_kernel, "mix_ffn"
        half = pl.BlockSpec((1, tm, D_ATTN), row)
        in_specs = [x_spec, half, half, mod_spec, _resident((D_MODEL, D_MODEL)), vec, vec, vec] + ffn_specs
        args = (x, mix[0], mix[1], mod, *mix[2:], wi_c, wo, ln_g, ln_b)
        scratch.append(pltpu.VMEM((nsub, FFN_SUB, D_MODEL), F32))
    return pl.pallas_call(
        body,
        out_shape=jax.ShapeDtypeStruct(x.shape, F32),
        grid=(b, l // tm),
        in_specs=in_specs,
        out_specs=x_spec,
        scratch_shapes=scratch,
        compiler_params=_params(("arbitrary", "arbitrary"), 60),
        name=name,
    )(*args)


def _rope_slab(t, cos, sin_up, sin_dn):
    return (t * cos + pltpu.roll(t, ROT_DIM // 2, 1) * sin_up
            + pltpu.roll(t, LANES - ROT_DIM // 2, 1) * sin_dn)


def _in_proj_kernel(x_ref, xp_ref, xn_ref, mod_ref, w_ref, cw_ref, cb_ref, cos_ref, sup_ref, sdn_ref,
                    q_ref, kd_ref, vd_ref, p_ref, x0_ref):
    j = pl.program_id(1)
    tm = x_ref.shape[1]
    m = mod_ref[0]
    x = jnp.concatenate([xp_ref[0], x_ref[0], xn_ref[0]], axis=0)
    h = (x * (1.0 + m[4:5]) + m[3:4]).astype(BF16)
    inner = slice(HALO, HALO + tm)
    first_tile, last_tile = j == 0, j == pl.num_programs(1) - 1

    def conv(col):
        lanes = slice(col, col + CONV_LANES)
        z = _dot(h, w_ref[:, D_ATTN + 2 * KV_WIDTH + col:D_ATTN + 2 * KV_WIDTH + col + CONV_LANES])
        z = jnp.concatenate([jnp.where(first_tile, 0.0, z[:HALO]), z[inner],
                             jnp.where(last_tile, 0.0, z[HALO + tm:])], axis=0)
        return (pltpu.roll(z, 1, 0)[inner] * cw_ref[0:1, lanes] + z[inner] * cw_ref[1:2, lanes]
                + pltpu.roll(z, tm + 2 * HALO - 1, 0)[inner] * cw_ref[2:3, lanes] + cb_ref[:, lanes])

    z = _dot(h[inner], w_ref[:, :D_ATTN + 2 * KV_WIDTH])
    cos, sup, sdn = cos_ref[...], sup_ref[...], sdn_ref[...]
    for s in range(D_ATTN // LANES):
        q = _rope_slab(z[:, s * LANES:(s + 1) * LANES], cos, sup, sdn)
        q_ref[0, :, s * LANES:(s + 1) * LANES] = (q * Q_SCALE).astype(BF16)
    k = _rope_slab(z[:, D_ATTN:D_ATTN + KV_WIDTH], cos, sup, sdn)
    v = z[:, D_ATTN + KV_WIDTH:D_ATTN + 2 * KV_WIDTH]
    lane = lax.broadcasted_iota(jnp.int32, k.shape, 1)
    first = lane < HEAD_DIM
    for t, ref in ((k, kd_ref), (v, vd_ref)):
        swapped = pltpu.roll(t, HEAD_DIM, 1)
        slabs = (jnp.where(first, t, 0.0), jnp.where(first, 0.0, swapped),
                 jnp.where(first, swapped, 0.0), jnp.where(first, 0.0, t))
        for i, slab in enumerate(slabs):
            ref[0, :, i * LANES:(i + 1) * LANES] = slab.astype(BF16)
    for t in range(D_HY // CONV_LANES):
        lanes = slice(t * CONV_LANES, (t + 1) * CONV_LANES)
        x0, x1, hv = (conv(g * D_HY + t * CONV_LANES) for g in range(3))
        p_ref[0, :, lanes] = (hv * x1).astype(BF16)
        x0_ref[0, :, lanes] = x0.astype(BF16)


def _in_proj(x, mod, w_in, conv_w, conv_b, rope):
    b, l, _ = x.shape
    tm = min(ROW_TILE, l)
    per = tm // HALO
    last = l // HALO - 1
    row = lambda i, j: (i, j, 0)
    tab = pl.BlockSpec((tm, LANES), lambda i, j: (j, 0))
    return pl.pallas_call(
        _in_proj_kernel,
        out_shape=(
            jax.ShapeDtypeStruct((b, l, D_ATTN), BF16),
            jax.ShapeDtypeStruct((b, l, KV_SLABS), BF16),
            jax.ShapeDtypeStruct((b, l, KV_SLABS), BF16),
            jax.ShapeDtypeStruct((b, l, D_HY), BF16),
            jax.ShapeDtypeStruct((b, l, D_HY), BF16),
        ),
        grid=(b, l // tm),
        in_specs=[
            pl.BlockSpec((1, tm, D_MODEL), row),
            pl.BlockSpec((1, HALO, D_MODEL), lambda i, j: (i, jnp.maximum(j * per - 1, 0), 0)),
            pl.BlockSpec((1, HALO, D_MODEL), lambda i, j: (i, jnp.minimum((j + 1) * per, last), 0)),
            pl.BlockSpec((1, N_MOD, D_MODEL), lambda i, j: (i, 0, 0)),
            _resident((D_MODEL, N_IN)),
            _resident((3, 3 * D_HY)),
            _resident((1, 3 * D_HY)),
            tab, tab, tab,
        ],
        out_specs=(
            pl.BlockSpec((1, tm, D_ATTN), row),
            pl.BlockSpec((1, tm, KV_SLABS), row),
            pl.BlockSpec((1, tm, KV_SLABS), row),
            pl.BlockSpec((1, tm, D_HY), row),
            pl.BlockSpec((1, tm, D_HY), row),
        ),
        compiler_params=_params(("arbitrary", "arbitrary"), 48),
        name="in_proj",
    )(x, x, x, mod, w_in, conv_w, conv_b.reshape(1, -1), *rope)


def _rope_tables(l):
    half = ROT_DIM // 2
    inv = ROPE_THETA ** (-jnp.arange(0, ROT_DIM, 2, dtype=F32) / ROT_DIM)
    ang = jnp.arange(l, dtype=F32)[:, None] * inv[None]
    cos, sin = jnp.cos(ang), jnp.sin(ang)
    rest = HEAD_DIM - ROT_DIM
    cos_h = jnp.concatenate([cos, cos, jnp.ones((l, rest), F32)], axis=1)
    up_h = jnp.concatenate([jnp.zeros((l, half), F32), sin, jnp.zeros((l, rest), F32)], axis=1)
    dn_h = jnp.concatenate([-sin, jnp.zeros((l, half + rest), F32)], axis=1)
    reps = LANES // HEAD_DIM
    return tuple(jnp.tile(t, (1, reps)) for t in (cos_h, up_h, dn_h))


def _attn_kernel(sink_ref, q_ref, kd_ref, vd_ref, o_ref, *, seq):
    j = pl.program_id(1)
    first = lax.broadcasted_iota(jnp.int32, (BAND, LANES), 1) < HEAD_DIM
    out_first = lax.broadcasted_iota(jnp.int32, (Q_BLOCK, LANES), 1) < HEAD_DIM
    ones_ext = jnp.concatenate([jnp.where(first, 1.0, 0.0), jnp.where(first, 0.0, 1.0)], axis=0).astype(BF16)
    def band(blk):
        q0 = j * q_ref.shape[1] + blk * Q_BLOCK
        start = pl.multiple_of(jnp.clip(q0 - Q_BLOCK, 0, seq - BAND), Q_BLOCK)
        qpos = q0 + lax.broadcasted_iota(jnp.int32, (Q_BLOCK, BAND), 0)
        kpos = start + lax.broadcasted_iota(jnp.int32, (Q_BLOCK, BAND), 1)
        return start, jnp.where(jnp.abs(qpos - kpos) <= WINDOW, 0.0, NEG_BIG).astype(F32)

    def scores(blk, kh, start):
        kb = jnp.concatenate([kd_ref[0, pl.ds(start, BAND), (2 * kh + e) * LANES:(2 * kh + e + 1) * LANES]
                              for e in range(2)], axis=0)
        q = jnp.concatenate([q_ref[0, blk * Q_BLOCK:(blk + 1) * Q_BLOCK, (2 * kh + r) * LANES:(2 * kh + r + 1) * LANES]
                             for r in range(2)], axis=0)
        return lax.dot_general(q, kb, (((1,), (1,)), ((), ())), preferred_element_type=F32)

    def finish(blk, kh, start, bias, s):
        vb = jnp.concatenate([vd_ref[0, pl.ds(start, BAND), (2 * kh + e) * LANES:(2 * kh + e + 1) * LANES]
                              for e in range(2)], axis=0)
        probs, sink_terms = [], []
        for r in range(2):
            pr, sk = [], []
            for e in range(2):
                sink = sink_ref[4 * kh + 2 * r + e] * LOG2E
                se = s[r * Q_BLOCK:(r + 1) * Q_BLOCK, e * BAND:(e + 1) * BAND] + bias
                mx = jnp.maximum(jnp.max(se, axis=1, keepdims=True), sink)
                pr.append(jnp.exp2(se - mx).astype(BF16))
                sk.append(jnp.exp2(sink - mx))
            probs.append(jnp.concatenate(pr, axis=1))
            sink_terms.append(jnp.where(out_first, sk[0], sk[1]))
        o = _dot(jnp.concatenate(probs, axis=0), jnp.concatenate([vb, ones_ext], axis=1))
        for r in range(2):
            part = o[r * Q_BLOCK:(r + 1) * Q_BLOCK]
            den = part[:, LANES:] + sink_terms[r]
            o_ref[0, blk * Q_BLOCK:(blk + 1) * Q_BLOCK, (2 * kh + r) * LANES:(2 * kh + r + 1) * LANES] = (
                part[:, :LANES] / den).astype(o_ref.dtype)

    units = [(blk, kh) for blk in range(q_ref.shape[1] // Q_BLOCK) for kh in range(N_KV_HEADS)]
    bands = {}

    def start_unit(blk, kh):
        if blk not in bands:
            bands[blk] = band(blk)
        return scores(blk, kh, bands[blk][0])

    s = start_unit(*units[0])
    for i, (blk, kh) in enumerate(units):
        s_next = start_unit(*units[i + 1]) if i + 1 < len(units) else None
        finish(blk, kh, *bands[blk], s)
        s = s_next


def _attention(q, kd, vd, sink):
    b, l, _ = q.shape
    tq = min(Q_TILE, l)
    return pl.pallas_call(
        functools.partial(_attn_kernel, seq=l),
        out_shape=jax.ShapeDtypeStruct((b, l, D_ATTN), BF16),
        grid=(b, l // tq),
        in_specs=[
            pl.BlockSpec(memory_space=pltpu.SMEM),
            pl.BlockSpec((1, tq, D_ATTN), lambda i, j: (i, j, 0)),
            pl.BlockSpec((1, l, KV_SLABS), lambda i, j: (i, 0, 0)),
            pl.BlockSpec((1, l, KV_SLABS), lambda i, j: (i, 0, 0)),
        ],
        out_specs=pl.BlockSpec((1, tq, D_ATTN), lambda i, j: (i, j, 0)),
        compiler_params=_params(("arbitrary", "arbitrary"), 32),
        name="window_attn",
    )(sink, q, kd, vd)


def _filter_kernel(z_ref, w1_ref, b1_ref, w2_ref, b2_ref, w3_ref, b3_ref, w4_ref, fr_ref, dec_ref,
                   e_ref, d_ref):
    z = z_ref[...]
    fr = fr_ref[...]
    h = jnp.sin(fr * (_dot3(z, w1_ref[...]) + b1_ref[...]))
    h = jnp.sin(fr * (_dot3(h, w2_ref[...]) + b2_ref[...]))
    h = jnp.sin(fr * (_dot3(h, w3_ref[...]) + b3_ref[...]))
    filt = _dot3(h, w4_ref[...]) * jnp.exp(-z[:, 0:1] * jnp.abs(dec_ref[...]))
    fwd, bwd = filt[:, :D_HY], filt[:, D_HY:]
    e_ref[...] = (fwd + bwd).astype(BF16)
    d_ref[...] = (fwd - bwd).astype(BF16)


def _filter_features(l):
    t = jnp.linspace(0.0, 1.0, l, dtype=F32)[:, None]
    n_bands = (FILTER_EMB - 1) // 2
    w = 2.0 * math.pi * jnp.arange(l, dtype=F32)[:, None] / l
    fb = jnp.linspace(1e-4, n_bands - 1, n_bands, dtype=F32)[None]
    z = jnp.concatenate([t, jnp.cos(fb * w), -jnp.sin(fb * w)], axis=-1)
    return jnp.pad(z, ((0, 0), (0, FILTER_HID - FILTER_EMB)))


def _filters(z, w1, b1, w2, b2, w3, b3, w4, freq, decay):
    l = z.shape[0]
    tl = min(ROW_TILE, l)
    w1p = jnp.pad(w1, ((0, FILTER_HID - FILTER_EMB), (0, 0)))
    vec = lambda a: a.reshape(1, -1)
    out = jax.ShapeDtypeStruct((l, D_HY), BF16)
    tile = pl.BlockSpec((tl, D_HY), lambda i: (i, 0))
    return pl.pallas_call(
        _filter_kernel,
        out_shape=(out, out),
        grid=(l // tl,),
        in_specs=[
            pl.BlockSpec((tl, FILTER_HID), lambda i: (i, 0)),
            _resident((FILTER_HID, FILTER_HID)), _resident((1, FILTER_HID)),
            _resident((FILTER_HID, FILTER_HID)), _resident((1, FILTER_HID)),
            _resident((FILTER_HID, FILTER_HID)), _resident((1, FILTER_HID)),
            _resident((FILTER_HID, 2 * D_HY)), _resident((1, FILTER_HID)), _resident((1, 2 * D_HY)),
        ],
        out_specs=(tile, tile),
        compiler_params=_params(("arbitrary",), 32),
        name="hyena_filter",
    )(z, w1p, vec(b1), w2, vec(b2), w3, vec(b3), w4, vec(freq), vec(decay))


def _fft_matrices(l):
    n = 2 * l
    n2 = FFT_N2
    n1 = n // n2
    kk = n1 // 2 + 1
    kh = jnp.arange(kk, dtype=jnp.int32)
    t1 = jnp.arange(n1 // 2, dtype=jnp.int32)
    ang1 = ((kh[:, None] * t1[None, :]) % n1).astype(F32) * (2.0 * math.pi / n1)
    f1 = jnp.stack([jnp.cos(ang1), jnp.sin(ang1)], axis=1).reshape(2 * kk, n1 // 2)
    eye = jnp.eye(ROW_GROUP, dtype=F32)
    kron = (f1[:, None, :, None] * eye[None, :, None, :]).reshape(2 * kk * ROW_GROUP, (n1 // 2) * ROW_GROUP)
    k1 = jnp.arange(n1, dtype=jnp.int32)
    k2 = jnp.arange(n2 // 2, dtype=jnp.int32)
    t2 = jnp.arange(n2, dtype=jnp.int32)
    ang_a = ((k1[:, None] * t2[None, :]) % n).astype(F32) * (2.0 * math.pi / n)
    ang_b = ((k2[:, None] * t2[None, :]) % n2).astype(F32) * (2.0 * math.pi / n2)
    ca, sa = jnp.cos(ang_a)[:, None, :], jnp.sin(ang_a)[:, None, :]
    cb, sb = jnp.cos(ang_b)[None], jnp.sin(ang_b)[None]
    c, s = ca * cb - sa * sb, sa * cb + ca * sb
    re_rows = jnp.concatenate([c, -s], axis=2)
    im_rows = jnp.concatenate([s, c], axis=2)
    nyq = jnp.concatenate([jnp.where(t2 % 2 == 0, 1.0, -1.0).astype(F32), jnp.zeros((n2,), F32)])
    im_rows = im_rows.at[0, 0, :].set(nyq)
    conj = jnp.concatenate([jnp.ones((n2,), F32), -jnp.ones((n2,), F32)])
    mate = (n1 - kh) % n1
    has_mate = ((kh > 0) & (kh < n1 // 2))[:, None, None]
    re_mate = jnp.where(has_mate, re_rows[mate] * conj, 0.0)
    im_mate = jnp.where(has_mate, im_rows[mate] * conj, 0.0)
    g = jnp.concatenate([re_rows[:kk], re_mate, im_rows[:kk], im_mate], axis=1)
    return dict(kron=kron.astype(BF16), kron_t=kron.T.astype(BF16),
                g=g.astype(BF16), g_t=g.transpose(0, 2, 1).astype(BF16))


def _fft_stage1(x_ref, kron_ref, b_scr):
    n1h, n2, c = x_ref.shape

    def group(g, carry):
        grp = _row_group(g)
        x = x_ref[:, grp, :].reshape(n1h * ROW_GROUP, c)
        b_scr[:, grp, :] = _dot(kron_ref[...], x).astype(BF16).reshape(b_scr.shape[0], ROW_GROUP, c)
        return carry

    lax.fori_loop(0, n2 // ROW_GROUP, group, 0, unroll=FFT_UNROLL)


def _row_group(g):
    return pl.ds(pl.multiple_of(g * ROW_GROUP, ROW_GROUP), ROW_GROUP)


def _fft_stage2(g_ref, b_scr, k):
    n2, c = b_scr.shape[1:]
    return _dot(g_ref[k], b_scr[pl.ds(2 * k, 2)].reshape(2 * n2, c))


def _spectrum_kernel(e_ref, d_ref, kron_ref, g_ref, ka_ref, kb_ref, ka2_ref, be_scr, bd_scr, *, seq):
    half = ka_ref.shape[1]
    _fft_stage1(e_ref, kron_ref, be_scr)
    _fft_stage1(d_ref, kron_ref, bd_scr)
    first_row = lax.broadcasted_iota(jnp.int32, (half, 1), 0) == 0

    def freq(k, carry):
        n2, c = be_scr.shape[1:]
        be = be_scr[pl.ds(2 * k, 2)].reshape(2 * n2, c)
        bd = bd_scr[pl.ds(2 * k, 2)].reshape(2 * n2, c)
        re = _dot(g_ref[k, :half, :], be)
        im = _dot(g_ref[k, half:, :], bd)
        nyq = _dot(g_ref[k, half:half + ROW_GROUP, :], be)[0:1]
        is_dc = first_row & (k == 0)
        ka = re * jnp.where(is_dc, 0.5 / seq, 1.0 / seq)
        ka_ref[k] = ka
        kb_ref[k] = jnp.where(is_dc, 0.0, im * (1.0 / seq))
        ka2_ref[k] = jnp.where(is_dc, nyq * (0.5 / seq), ka)
        return carry

    lax.fori_loop(0, ka_ref.shape[0], freq, 0, unroll=FFT_UNROLL)


def _spectrum(mats, e, d):
    l = e.shape[0]
    kk, n2 = mats["g"].shape[0], FFT_N2
    n1h = l // n2
    cw = LANES
    out = jax.ShapeDtypeStruct((kk, n2, D_HY), F32)
    src = pl.BlockSpec((n1h, n2, cw), lambda h: (0, 0, h))
    dst = pl.BlockSpec((kk, n2, cw), lambda h: (0, 0, h))
    view = lambda a: a.reshape(n1h, n2, D_HY)
    return pl.pallas_call(
        functools.partial(_spectrum_kernel, seq=l),
        out_shape=(out, out, out),
        grid=(D_HY // cw,),
        in_specs=[src, src, _resident(mats["kron"].shape), _resident(mats["g"].shape)],
        out_specs=(dst, dst, dst),
        scratch_shapes=[pltpu.VMEM((2 * kk, n2, cw), BF16), pltpu.VMEM((2 * kk, n2, cw), BF16)],
        compiler_params=_params(("arbitrary",), 56),
        name="hyena_spectrum",
    )(view(e), view(d), mats["kron"], mats["g"])


def _long_conv_kernel(p_ref, x0_ref, kron_ref, kron_t_ref, g_ref, gt_ref, ka_ref, kb_ref, ka2_ref, bias_ref,
                      o_ref, b_scr):
    n1h, n2, c = p_ref.shape[1:]
    half = g_ref.shape[1] // 2
    _fft_stage1(p_ref.at[0], kron_ref, b_scr)

    def freq(k, carry):
        u = _fft_stage2(g_ref, b_scr, k)
        ure, uim = u[:half], u[half:]
        ka, kb, ka2 = ka_ref[k], kb_ref[k], ka2_ref[k]
        y = jnp.concatenate([ure * ka - uim * kb, ure * kb + uim * ka2], axis=0).astype(BF16)
        b_scr[pl.ds(2 * k, 2)] = _dot(gt_ref[k], y).astype(BF16).reshape(2, n2, c)
        return carry

    lax.fori_loop(0, g_ref.shape[0], freq, 0, unroll=FFT_UNROLL)
    bias = bias_ref[...]

    def group(g, carry):
        grp = _row_group(g)
        z = b_scr[:, grp, :].reshape(b_scr.shape[0] * ROW_GROUP, c)
        y = _dot(kron_t_ref[...], z).reshape(n1h, ROW_GROUP, c)
        gated = (y + p_ref[0, :, grp, :].astype(F32) * bias) * x0_ref[0, :, grp, :].astype(F32)
        o_ref[0, :, grp, :] = gated.astype(BF16)
        return carry

    lax.fori_loop(0, n2 // ROW_GROUP, group, 0, unroll=FFT_UNROLL)


def _long_conv(p, x0, mats, ka, kb, ka2, bias):
    b, l, _ = p.shape
    kk, n2 = mats["g"].shape[0], FFT_N2
    n1h = l // n2
    cw = FFT_LANES
    view = lambda a: a.reshape(b, n1h, n2, D_HY)
    tile = pl.BlockSpec((1, n1h, n2, cw), lambda h, i: (i, 0, 0, h))
    spec = pl.BlockSpec((kk, n2, cw), lambda h, i: (0, 0, h), pipeline_mode=pl.Buffered(1))
    out = pl.pallas_call(
        _long_conv_kernel,
        out_shape=jax.ShapeDtypeStruct((b, n1h, n2, D_HY), BF16),
        grid=(D_HY // cw, b),
        in_specs=[
            tile, tile,
            _resident(mats["kron"].shape), _resident(mats["kron_t"].shape),
            _resident(mats["g"].shape), _resident(mats["g_t"].shape),
            spec, spec, spec,
            pl.BlockSpec((1, cw), lambda h, i: (0, h)),
        ],
        out_specs=tile,
        scratch_shapes=[pltpu.VMEM((2 * kk, n2, cw), BF16)],
        compiler_params=_params(("arbitrary", "arbitrary"), 60),
        name="hyena_long_conv",
    )(view(p), view(x0), mats["kron"], mats["kron_t"], mats["g"], mats["g_t"], ka, kb, ka2, bias.reshape(1, -1))
    return out.reshape(b, l, D_HY)


def _chunk_ffn_weights(wi, wo):
    nck = D_FF // FF_CHUNK
    g = wi[:, :D_FF].reshape(D_MODEL, nck, FF_CHUNK)
    u = wi[:, D_FF:].reshape(D_MODEL, nck, FF_CHUNK)
    wi_c = jnp.concatenate([g, u], axis=2).transpose(1, 0, 2).astype(BF16)
    return wi_c, wo.astype(BF16)


def _trunk(x, mod, weights, p):
    l = x.shape[1]
    depth = mod.shape[0]
    rope = _rope_tables(l)
    feats = _filter_features(l)
    mats = _fft_matrices(l)
    vec = lambda a: a.reshape(1, -1)
    for i in range(depth):
        w = weights[i]
        m = mod[i]
        x = _ffn(x, m, w["wi1"], w["wo1"], vec(p["ln_g"][i, 0]), vec(p["ln_b"][i, 0]))
        q, kd, vd, gated, x0 = _in_proj(x, m, w["w_in"], p["hy_conv_w"][i], p["hy_conv_b"][i], rope)
        o_attn = _attention(q, kd, vd, p["sink"][i])
        e, d = _filters(feats, p["hy_w1"][i], p["hy_b1"][i], p["hy_w2"][i], p["hy_b2"][i],
                        p["hy_w3"][i], p["hy_b3"][i], p["hy_w4"][i], p["hy_freq"][i], p["hy_decay"][i])
        ka, kb, ka2 = _spectrum(mats, e, d)
        o_hy = _long_conv(gated, x0, mats, ka, kb, ka2, p["hy_bias"][i])
        mix = (o_attn, o_hy, w["w_out"], vec(p["grp_norm_g"][i]), vec(p["ln_g"][i, 1]), vec(p["ln_b"][i, 1]))
        x = _ffn(x, m, w["wi2"], w["wo2"], vec(p["ln_g"][i, 2]), vec(p["ln_b"][i, 2]), mix)
    return x


def kernel(x_prompt, x_sample, c_prompt, c_sample, ada_w, ada_b, ffn1_wi, ffn1_wo, ffn2_wi, ffn2_wo, ln_g, ln_b, w_in, w_out, sink, grp_norm_g, hy_conv_w, hy_conv_b, hy_w1, hy_b1, hy_w2, hy_b2, hy_w3, hy_b3, hy_w4, hy_freq, hy_decay, hy_bias):
    p = dict(ln_g=ln_g, ln_b=ln_b, sink=sink, grp_norm_g=grp_norm_g, hy_conv_w=hy_conv_w,
             hy_conv_b=hy_conv_b, hy_w1=hy_w1, hy_b1=hy_b1, hy_w2=hy_w2, hy_b2=hy_b2, hy_w3=hy_w3,
             hy_b3=hy_b3, hy_w4=hy_w4, hy_freq=hy_freq, hy_decay=hy_decay, hy_bias=hy_bias)
    depth = ada_w.shape[0]
    weights = []
    for i in range(depth):
        wi1, wo1 = _chunk_ffn_weights(ffn1_wi[i], ffn1_wo[i])
        wi2, wo2 = _chunk_ffn_weights(ffn2_wi[i], ffn2_wo[i])
        weights.append(dict(wi1=wi1, wo1=wo1, wi2=wi2, wo2=wo2,
                            w_in=w_in[i].astype(BF16), w_out=w_out[i].astype(BF16)))
    nb = c_prompt.shape[0]
    mod = _modulation(jnp.concatenate([c_prompt, c_sample], axis=0), ada_w, ada_b)
    y_prompt = _trunk(x_prompt, mod[:, :nb], weights, p)
    y_sample = _trunk(x_sample, mod[:, nb:], weights, p)
    return (y_prompt, y_sample)
```

```python
import functools
import math

import jax
import jax.numpy as jnp
from jax import lax
from jax.experimental import pallas as pl
from jax.experimental.pallas import tpu as pltpu

F32 = jnp.float32
BF16 = jnp.bfloat16

D_MODEL = 1024
D_ATTN = 512
D_HY = 512
HEAD_DIM = 64
N_Q_HEADS = D_ATTN // HEAD_DIM
N_KV_HEADS = 2
KV_WIDTH = N_KV_HEADS * HEAD_DIM
ROT_DIM = HEAD_DIM // 4
ROPE_THETA = 500000.0
WINDOW = 128
FILTER_EMB = 33
FILTER_HID = 64
D_FF = 2816
N_IN = D_ATTN + 2 * KV_WIDTH + 3 * D_HY
N_MOD = 9
DEPTH = 4
ALPHA = float((2 * DEPTH) ** 0.25)
LN_EPS = 1e-5
RMS_EPS = 1e-6
NEG_BIG = -1e30
LOG2E = math.log2(math.e)
Q_SCALE = HEAD_DIM ** -0.5 * LOG2E

LANES = 128
KV_SLABS = 2 * N_KV_HEADS * LANES
VMEM_BYTES = 64 * 1024 * 1024
MIB = 1024 * 1024

ROW_TILE = 1024
FFN_TILE = 1024
FFN_SUB = 512
FF_CHUNK = 256
LN_ROWS = 256
INTERLEAVE_EVERY = 2
Q_TILE = 2048
Q_BLOCK = 128
BAND = 3 * Q_BLOCK
FFT_N2 = 256
FFT_LANES = 256
ROW_GROUP = 16
FFT_UNROLL = 8
FREQ_UNROLL = 16
MOD_N_TILE = 1536
HALO = 16
CONV_LANES = 256


def _params(semantics, vmem_mib):
    return pltpu.CompilerParams(dimension_semantics=semantics,
                                vmem_limit_bytes=min(vmem_mib * MIB, VMEM_BYTES - 4 * MIB))


def _resident(shape):
    nd = len(shape)
    return pl.BlockSpec(shape, lambda *_: (0,) * nd, pipeline_mode=pl.Buffered(1))


def _dot(a, b):
    return jnp.dot(a, b, preferred_element_type=F32)


def _split(a):
    hi = a.astype(BF16)
    lo = (a - hi.astype(F32)).astype(BF16)
    return hi, lo


def _dot3(a, b):
    a_hi, a_lo = _split(a)
    b_hi, b_lo = _split(b)
    return _dot(a_hi, b_hi) + _dot(a_hi, b_lo) + _dot(a_lo, b_hi)


def _layer_norm(y, g, b):
    mu = jnp.mean(y, axis=-1, keepdims=True)
    d = y - mu
    var = jnp.mean(d * d, axis=-1, keepdims=True)
    return d * lax.rsqrt(var + LN_EPS) * g + b


def _rms_norm(y, g):
    ms = jnp.mean(y * y, axis=-1, keepdims=True)
    return y * lax.rsqrt(ms + RMS_EPS) * g


def _mod_kernel(c_ref, w_ref, b_ref, o_ref):
    c = c_ref[...]
    a = c * jax.nn.sigmoid(c)
    o_ref[0] = _dot3(a, w_ref[0]) + b_ref[0]


def _modulation(c, ada_w, ada_b):
    depth = ada_w.shape[0]
    b = c.shape[0]
    n = N_MOD * D_MODEL
    out = pl.pallas_call(
        _mod_kernel,
        out_shape=jax.ShapeDtypeStruct((depth, b, n), F32),
        grid=(depth, n // MOD_N_TILE),
        in_specs=[
            pl.BlockSpec((b, D_MODEL), lambda l, j: (0, 0)),
            pl.BlockSpec((1, D_MODEL, MOD_N_TILE), lambda l, j: (l, 0, j)),
            pl.BlockSpec((1, 1, MOD_N_TILE), lambda l, j: (l, 0, j)),
        ],
        out_specs=pl.BlockSpec((1, b, MOD_N_TILE), lambda l, j: (l, 0, j)),
        compiler_params=_params(("arbitrary", "arbitrary"), 40),
        name="adaln_mod",
    )(c, ada_w, ada_b.reshape(depth, 1, n))
    return out.reshape(depth, b, N_MOD, D_MODEL)


def _ln_pieces():
    return [slice(r * LN_ROWS, (r + 1) * LN_ROWS) for r in range(FFN_SUB // LN_ROWS)]


def _modulate_rows(x, m, mod_row):
    return (x * (1.0 + m[mod_row + 1:mod_row + 2]) + m[mod_row:mod_row + 1]).astype(BF16)


def _swiglu_up(h_rows, a_rows, wi_ref, interleave):
    pending = list(interleave)
    for c in range(D_FF // FF_CHUNK):
        gu = _dot(h_rows[...], wi_ref[c])
        g, u = gu[:, :FF_CHUNK], gu[:, FF_CHUNK:]
        a_rows[:, c * FF_CHUNK:(c + 1) * FF_CHUNK] = (g * jax.nn.sigmoid(g) * u).astype(BF16)
        if pending and c % INTERLEAVE_EVERY == 1:
            pending.pop(0)()
    assert not pending


def _swiglu_down(x_rows, a_rows, o_rows, gate, wo_ref, g_ref, b_ref, piece):
    y = ALPHA * x_rows[piece, :] + (0.5 * (1.0 + gate)) * _dot(a_rows[piece, :], wo_ref[...])
    o_rows[piece, :] = _layer_norm(y, g_ref[...], b_ref[...])


def _run_sub_tiles(nsub, prepare, up, finish):
    pieces = _ln_pieces()
    for piece in pieces:
        prepare(0, piece)
    for s in range(nsub):
        tasks = [functools.partial(finish, s - 1, piece) for piece in pieces] if s > 0 else []
        tasks += [functools.partial(prepare, s + 1, piece) for piece in pieces] if s + 1 < nsub else []
        up(s, tasks)
    for piece in pieces:
        finish(nsub - 1, piece)


def _ffn_kernel(x_ref, mod_ref, wi_ref, wo_ref, g_ref, b_ref, o_ref, h_scr, a_scr):
    m = mod_ref[0]

    def prepare(s, piece):
        h_scr[s, piece, :] = _modulate_rows(x_ref[0, pl.ds(s * FFN_SUB + piece.start, LN_ROWS), :], m, 0)

    def up(s, tasks):
        _swiglu_up(h_scr.at[s], a_scr.at[s], wi_ref, tasks)

    def finish(s, piece):
        rows = pl.ds(s * FFN_SUB, FFN_SUB)
        _swiglu_down(x_ref.at[0, rows], a_scr.at[s], o_ref.at[0, rows], m[2:3], wo_ref, g_ref, b_ref, piece)

    _run_sub_tiles(h_scr.shape[0], prepare, up, finish)


def _mix_ffn_kernel(x_ref, oa_ref, oh_ref, mod_ref, w_out_ref, gn_ref, gm_ref, bm_ref, wi_ref, wo_ref, g_ref, b_ref,
                    o_ref, h_scr, a_scr, x_scr):
    m = mod_ref[0]
    gn = gn_ref[...]

    def prepare(s, piece):
        rows = pl.ds(s * FFN_SUB + piece.start, LN_ROWS)
        a = _rms_norm(oa_ref[0, rows, :].astype(F32), gn[:, :D_ATTN]).astype(BF16)
        h = _rms_norm(oh_ref[0, rows, :].astype(F32), gn[:, D_ATTN:]).astype(BF16)
        o = _dot(a, w_out_ref[:D_ATTN, :]) + _dot(h, w_out_ref[D_ATTN:, :])
        x = _layer_norm(ALPHA * x_ref[0, rows, :] + (1.0 + m[5:6]) * o, gm_ref[...], bm_ref[...])
        x_scr[s, piece, :] = x
        h_scr[s, piece, :] = _modulate_rows(x, m, 6)

    def up(s, tasks):
        _swiglu_up(h_scr.at[s], a_scr.at[s], wi_ref, tasks)

    def finish(s, piece):
        _swiglu_down(x_scr.at[s], a_scr.at[s], o_ref.at[0, pl.ds(s * FFN_SUB, FFN_SUB)], m[8:9],
                     wo_ref, g_ref, b_ref, piece)

    _run_sub_tiles(h_scr.shape[0], prepare, up, finish)


def _ffn(x, mod, wi_c, wo, ln_g, ln_b, mix=None):
    b, l, _ = x.shape
    tm = min(FFN_TILE, l)
    nsub = tm // FFN_SUB
    nck = D_FF // FF_CHUNK
    row = lambda i, j: (i, j, 0)
    vec = _resident((1, D_MODEL))
    x_spec = pl.BlockSpec((1, tm, D_MODEL), row)
    mod_spec = pl.BlockSpec((1, N_MOD, D_MODEL), lambda i, j: (i, 0, 0))
    ffn_specs = [_resident((nck, D_MODEL, 2 * FF_CHUNK)), _resident((D_FF, D_MODEL)), vec, vec]
    scratch = [pltpu.VMEM((nsub, FFN_SUB, D_MODEL), BF16), pltpu.VMEM((nsub, FFN_SUB, D_FF), BF16)]
    if mix is None:
        body, name = _ffn_kernel, "ffn"
        in_specs = [x_spec, mod_spec] + ffn_specs
        args = (x, mod, wi_c, wo, ln_g, ln_b)
    else:
        body, name = _mix_ffn_kernel, "mix_ffn"
        half = pl.BlockSpec((1, tm, D_ATTN), row)
        in_specs = [x_spec, half, half, mod_spec, _resident((D_MODEL, D_MODEL)), vec, vec, vec] + ffn_specs
        args = (x, mix[0], mix[1], mod, *mix[2:], wi_c, wo, ln_g, ln_b)
        scratch.append(pltpu.VMEM((nsub, FFN_SUB, D_MODEL), F32))
    return pl.pallas_call(
        body,
        out_shape=jax.ShapeDtypeStruct(x.shape, F32),
        grid=(b, l // tm),
        in_specs=in_specs,
        out_specs=x_spec,
        scratch_shapes=scratch,
        compiler_params=_params(("arbitrary", "arbitrary"), 60),
        name=name,
    )(*args)


def _rope_slab(t, cos, sin_up, sin_dn):
    return (t * cos + pltpu.roll(t, ROT_DIM // 2, 1) * sin_up
            + pltpu.roll(t, LANES - ROT_DIM // 2, 1) * sin_dn)


def _in_proj_kernel(x_ref, xp_ref, xn_ref, mod_ref, w_ref, cw_ref, cb_ref, cos_ref, sup_ref, sdn_ref,
                    q_ref, kd_ref, vd_ref, p_ref, x0_ref):
    j = pl.program_id(1)
    tm = x_ref.shape[1]
    m = mod_ref[0]
    x = jnp.concatenate([xp_ref[0], x_ref[0], xn_ref[0]], axis=0)
    h = (x * (1.0 + m[4:5]) + m[3:4]).astype(BF16)
    inner = slice(HALO, HALO + tm)
    first_tile, last_tile = j == 0, j == pl.num_programs(1) - 1

    def conv(col):
        lanes = slice(col, col + CONV_LANES)
        z = _dot(h, w_ref[:, D_ATTN + 2 * KV_WIDTH + col:D_ATTN + 2 * KV_WIDTH + col + CONV_LANES])
        z = jnp.concatenate([jnp.where(first_tile, 0.0, z[:HALO]), z[inner],
                             jnp.where(last_tile, 0.0, z[HALO + tm:])], axis=0)
        return (pltpu.roll(z, 1, 0)[inner] * cw_ref[0:1, lanes] + z[inner] * cw_ref[1:2, lanes]
                + pltpu.roll(z, tm + 2 * HALO - 1, 0)[inner] * cw_ref[2:3, lanes] + cb_ref[:, lanes])

    z = _dot(h[inner], w_ref[:, :D_ATTN + 2 * KV_WIDTH])
    cos, sup, sdn = cos_ref[...], sup_ref[...], sdn_ref[...]
    for s in range(D_ATTN // LANES):
        q = _rope_slab(z[:, s * LANES:(s + 1) * LANES], cos, sup, sdn)
        q_ref[0, :, s * LANES:(s + 1) * LANES] = (q * Q_SCALE).astype(BF16)
    k = _rope_slab(z[:, D_ATTN:D_ATTN + KV_WIDTH], cos, sup, sdn)
    v = z[:, D_ATTN + KV_WIDTH:D_ATTN + 2 * KV_WIDTH]
    lane = lax.broadcasted_iota(jnp.int32, k.shape, 1)
    first = lane < HEAD_DIM
    for t, ref in ((k, kd_ref), (v, vd_ref)):
        swapped = pltpu.roll(t, HEAD_DIM, 1)
        slabs = (jnp.where(first, t, 0.0), jnp.where(first, 0.0, swapped),
                 jnp.where(first, swapped, 0.0), jnp.where(first, 0.0, t))
        for i, slab in enumerate(slabs):
            ref[0, :, i * LANES:(i + 1) * LANES] = slab.astype(BF16)
    for t in range(D_HY // CONV_LANES):
        lanes = slice(t * CONV_LANES, (t + 1) * CONV_LANES)
        x0, x1, hv = (conv(g * D_HY + t * CONV_LANES) for g in range(3))
        p_ref[0, :, lanes] = (hv * x1).astype(BF16)
        x0_ref[0, :, lanes] = x0.astype(BF16)


def _in_proj(x, mod, w_in, conv_w, conv_b, rope):
    b, l, _ = x.shape
    tm = min(ROW_TILE, l)
    per = tm // HALO
    last = l // HALO - 1
    row = lambda i, j: (i, j, 0)
    tab = pl.BlockSpec((tm, LANES), lambda i, j: (j, 0))
    return pl.pallas_call(
        _in_proj_kernel,
        out_shape=(
            jax.ShapeDtypeStruct((b, l, D_ATTN), BF16),
            jax.ShapeDtypeStruct((b, l, KV_SLABS), BF16),
            jax.ShapeDtypeStruct((b, l, KV_SLABS), BF16),
            jax.ShapeDtypeStruct((b, l, D_HY), BF16),
            jax.ShapeDtypeStruct((b, l, D_HY), BF16),
        ),
        grid=(b, l // tm),
        in_specs=[
            pl.BlockSpec((1, tm, D_MODEL), row),
            pl.BlockSpec((1, HALO, D_MODEL), lambda i, j: (i, jnp.maximum(j * per - 1, 0), 0)),
            pl.BlockSpec((1, HALO, D_MODEL), lambda i, j: (i, jnp.minimum((j + 1) * per, last), 0)),
            pl.BlockSpec((1, N_MOD, D_MODEL), lambda i, j: (i, 0, 0)),
            _resident((D_MODEL, N_IN)),
            _resident((3, 3 * D_HY)),
            _resident((1, 3 * D_HY)),
            tab, tab, tab,
        ],
        out_specs=(
            pl.BlockSpec((1, tm, D_ATTN), row),
            pl.BlockSpec((1, tm, KV_SLABS), row),
            pl.BlockSpec((1, tm, KV_SLABS), row),
            pl.BlockSpec((1, tm, D_HY), row),
            pl.BlockSpec((1, tm, D_HY), row),
        ),
        compiler_params=_params(("arbitrary", "arbitrary"), 48),
        name="in_proj",
    )(x, x, x, mod, w_in, conv_w, conv_b.reshape(1, -1), *rope)


def _rope_tables(l):
    half = ROT_DIM // 2
    inv = ROPE_THETA ** (-jnp.arange(0, ROT_DIM, 2, dtype=F32) / ROT_DIM)
    ang = jnp.arange(l, dtype=F32)[:, None] * inv[None]
    cos, sin = jnp.cos(ang), jnp.sin(ang)
    rest = HEAD_DIM - ROT_DIM
    cos_h = jnp.concatenate([cos, cos, jnp.ones((l, rest), F32)], axis=1)
    up_h = jnp.concatenate([jnp.zeros((l, half), F32), sin, jnp.zeros((l, rest), F32)], axis=1)
    dn_h = jnp.concatenate([-sin, jnp.zeros((l, half + rest), F32)], axis=1)
    reps = LANES // HEAD_DIM
    return tuple(jnp.tile(t, (1, reps)) for t in (cos_h, up_h, dn_h))


def _attn_kernel(sink_ref, q_ref, kd_ref, vd_ref, o_ref, *, seq):
    j = pl.program_id(1)
    first = lax.broadcasted_iota(jnp.int32, (BAND, LANES), 1) < HEAD_DIM
    out_first = lax.broadcasted_iota(jnp.int32, (Q_BLOCK, LANES), 1) < HEAD_DIM
    ones_ext = jnp.concatenate([jnp.where(first, 1.0, 0.0), jnp.where(first, 0.0, 1.0)], axis=0).astype(BF16)
    def band(blk):
        q0 = j * q_ref.shape[1] + blk * Q_BLOCK
        start = pl.multiple_of(jnp.clip(q0 - Q_BLOCK, 0, seq - BAND), Q_BLOCK)
        qpos = q0 + lax.broadcasted_iota(jnp.int32, (Q_BLOCK, BAND), 0)
        kpos = start + lax.broadcasted_iota(jnp.int32, (Q_BLOCK, BAND), 1)
        return start, jnp.where(jnp.abs(qpos - kpos) <= WINDOW, 0.0, NEG_BIG).astype(F32)

    def scores(blk, kh, start):
        kb = jnp.concatenate([kd_ref[0, pl.ds(start, BAND), (2 * kh + e) * LANES:(2 * kh + e + 1) * LANES]
                              for e in range(2)], axis=0)
        q = jnp.concatenate([q_ref[0, blk * Q_BLOCK:(blk + 1) * Q_BLOCK, (2 * kh + r) * LANES:(2 * kh + r + 1) * LANES]
                             for r in range(2)], axis=0)
        return lax.dot_general(q, kb, (((1,), (1,)), ((), ())), preferred_element_type=F32)

    def finish(blk, kh, start, bias, s):
        vb = jnp.concatenate([vd_ref[0, pl.ds(start, BAND), (2 * kh + e) * LANES:(2 * kh + e + 1) * LANES]
                              for e in range(2)], axis=0)
        probs, sink_terms = [], []
        for r in range(2):
            pr, sk = [], []
            for e in range(2):
                sink = sink_ref[4 * kh + 2 * r + e] * LOG2E
                se = s[r * Q_BLOCK:(r + 1) * Q_BLOCK, e * BAND:(e + 1) * BAND] + bias
                mx = jnp.maximum(jnp.max(se, axis=1, keepdims=True), sink)
                pr.append(jnp.exp2(se - mx).astype(BF16))
                sk.append(jnp.exp2(sink - mx))
            probs.append(jnp.concatenate(pr, axis=1))
            sink_terms.append(jnp.where(out_first, sk[0], sk[1]))
        o = _dot(jnp.concatenate(probs, axis=0), jnp.concatenate([vb, ones_ext], axis=1))
        for r in range(2):
            part = o[r * Q_BLOCK:(r + 1) * Q_BLOCK]
            den = part[:, LANES:] + sink_terms[r]
            o_ref[0, blk * Q_BLOCK:(blk + 1) * Q_BLOCK, (2 * kh + r) * LANES:(2 * kh + r + 1) * LANES] = (
                part[:, :LANES] / den).astype(o_ref.dtype)

    units = [(blk, kh) for blk in range(q_ref.shape[1] // Q_BLOCK) for kh in range(N_KV_HEADS)]
    bands = {}

    def start_unit(blk, kh):
        if blk not in bands:
            bands[blk] = band(blk)
        return scores(blk, kh, bands[blk][0])

    s = start_unit(*units[0])
    for i, (blk, kh) in enumerate(units):
        s_next = start_unit(*units[i + 1]) if i + 1 < len(units) else None
        finish(blk, kh, *bands[blk], s)
        s = s_next


def _attention(q, kd, vd, sink):
    b, l, _ = q.shape
    tq = min(Q_TILE, l)
    return pl.pallas_call(
        functools.partial(_attn_kernel, seq=l),
        out_shape=jax.ShapeDtypeStruct((b, l, D_ATTN), BF16),
        grid=(b, l // tq),
        in_specs=[
            pl.BlockSpec(memory_space=pltpu.SMEM),
            pl.BlockSpec((1, tq, D_ATTN), lambda i, j: (i, j, 0)),
            pl.BlockSpec((1, l, KV_SLABS), lambda i, j: (i, 0, 0)),
            pl.BlockSpec((1, l, KV_SLABS), lambda i, j: (i, 0, 0)),
        ],
        out_specs=pl.BlockSpec((1, tq, D_ATTN), lambda i, j: (i, j, 0)),
        compiler_params=_params(("arbitrary", "arbitrary"), 32),
        name="window_attn",
    )(sink, q, kd, vd)


def _filter_kernel(z_ref, w1_ref, b1_ref, w2_ref, b2_ref, w3_ref, b3_ref, w4_ref, fr_ref, dec_ref,
                   e_ref, d_ref):
    z = z_ref[...]
    fr = fr_ref[...]
    h = jnp.sin(fr * (_dot3(z, w1_ref[...]) + b1_ref[...]))
    h = jnp.sin(fr * (_dot3(h, w2_ref[...]) + b2_ref[...]))
    h = jnp.sin(fr * (_dot3(h, w3_ref[...]) + b3_ref[...]))
    filt = _dot3(h, w4_ref[...]) * jnp.exp(-z[:, 0:1] * jnp.abs(dec_ref[...]))
    fwd, bwd = filt[:, :D_HY], filt[:, D_HY:]
    e_ref[...] = (fwd + bwd).astype(BF16)
    d_ref[...] = (fwd - bwd).astype(BF16)


def _filter_features(l):
    t = jnp.linspace(0.0, 1.0, l, dtype=F32)[:, None]
    n_bands = (FILTER_EMB - 1) // 2
    w = 2.0 * math.pi * jnp.arange(l, dtype=F32)[:, None] / l
    fb = jnp.linspace(1e-4, n_bands - 1, n_bands, dtype=F32)[None]
    z = jnp.concatenate([t, jnp.cos(fb * w), -jnp.sin(fb * w)], axis=-1)
    return jnp.pad(z, ((0, 0), (0, FILTER_HID - FILTER_EMB)))


def _filters(z, w1, b1, w2, b2, w3, b3, w4, freq, decay):
    l = z.shape[0]
    tl = min(ROW_TILE, l)
    w1p = jnp.pad(w1, ((0, FILTER_HID - FILTER_EMB), (0, 0)))
    vec = lambda a: a.reshape(1, -1)
    out = jax.ShapeDtypeStruct((l, D_HY), BF16)
    tile = pl.BlockSpec((tl, D_HY), lambda i: (i, 0))
    return pl.pallas_call(
        _filter_kernel,
        out_shape=(out, out),
        grid=(l // tl,),
        in_specs=[
            pl.BlockSpec((tl, FILTER_HID), lambda i: (i, 0)),
            _resident((FILTER_HID, FILTER_HID)), _resident((1, FILTER_HID)),
            _resident((FILTER_HID, FILTER_HID)), _resident((1, FILTER_HID)),
            _resident((FILTER_HID, FILTER_HID)), _resident((1, FILTER_HID)),
            _resident((FILTER_HID, 2 * D_HY)), _resident((1, FILTER_HID)), _resident((1, 2 * D_HY)),
        ],
        out_specs=(tile, tile),
        compiler_params=_params(("arbitrary",), 32),
        name="hyena_filter",
    )(z, w1p, vec(b1), w2, vec(b2), w3, vec(b3), w4, vec(freq), vec(decay))


def _fft_matrices(l):
    n = 2 * l
    n2 = FFT_N2
    n1 = n // n2
    kk = n1 // 2 + 1
    kh = jnp.arange(kk, dtype=jnp.int32)
    t1 = jnp.arange(n1 // 2, dtype=jnp.int32)
    ang1 = ((kh[:, None] * t1[None, :]) % n1).astype(F32) * (2.0 * math.pi / n1)
    f1 = jnp.stack([jnp.cos(ang1), jnp.sin(ang1)], axis=1).reshape(2 * kk, n1 // 2)
    eye = jnp.eye(ROW_GROUP, dtype=F32)
    kron = (f1[:, None, :, None] * eye[None, :, None, :]).reshape(2 * kk * ROW_GROUP, (n1 // 2) * ROW_GROUP)
    k1 = jnp.arange(n1, dtype=jnp.int32)
    k2 = jnp.arange(n2 // 2, dtype=jnp.int32)
    t2 = jnp.arange(n2, dtype=jnp.int32)
    ang_a = ((k1[:, None] * t2[None, :]) % n).astype(F32) * (2.0 * math.pi / n)
    ang_b = ((k2[:, None] * t2[None, :]) % n2).astype(F32) * (2.0 * math.pi / n2)
    ca, sa = jnp.cos(ang_a)[:, None, :], jnp.sin(ang_a)[:, None, :]
    cb, sb = jnp.cos(ang_b)[None], jnp.sin(ang_b)[None]
    c, s = ca * cb - sa * sb, sa * cb + ca * sb
    re_rows = jnp.concatenate([c, -s], axis=2)
    im_rows = jnp.concatenate([s, c], axis=2)
    nyq = jnp.concatenate([jnp.where(t2 % 2 == 0, 1.0, -1.0).astype(F32), jnp.zeros((n2,), F32)])
    im_rows = im_rows.at[0, 0, :].set(nyq)
    conj = jnp.concatenate([jnp.ones((n2,), F32), -jnp.ones((n2,), F32)])
    mate = (n1 - kh) % n1
    has_mate = ((kh > 0) & (kh < n1 // 2))[:, None, None]
    re_mate = jnp.where(has_mate, re_rows[mate] * conj, 0.0)
    im_mate = jnp.where(has_mate, im_rows[mate] * conj, 0.0)
    g = jnp.concatenate([re_rows[:kk], re_mate, im_rows[:kk], im_mate], axis=1)
    return dict(kron=kron.astype(BF16), kron_t=kron.T.astype(BF16),
                g=g.astype(BF16), g_t=g.transpose(0, 2, 1).astype(BF16))


def _fft_stage1(x_ref, kron_ref, b_scr):
    n1h, n2, c = x_ref.shape

    def group(g, carry):
        grp = _row_group(g)
        x = x_ref[:, grp, :].reshape(n1h * ROW_GROUP, c)
        b_scr[:, grp, :] = _dot(kron_ref[...], x).astype(BF16).reshape(b_scr.shape[0], ROW_GROUP, c)
        return carry

    lax.fori_loop(0, n2 // ROW_GROUP, group, 0, unroll=FFT_UNROLL)


def _row_group(g):
    return pl.ds(pl.multiple_of(g * ROW_GROUP, ROW_GROUP), ROW_GROUP)


def _fft_stage2(g_ref, b_scr, k):
    n2, c = b_scr.shape[1:]
    return _dot(g_ref[k], b_scr[pl.ds(2 * k, 2)].reshape(2 * n2, c))


def _spectrum_kernel(e_ref, d_ref, kron_ref, g_ref, ka_ref, kb_ref, ka2_ref, be_scr, bd_scr, *, seq):
    half = ka_ref.shape[1]
    _fft_stage1(e_ref, kron_ref, be_scr)
    _fft_stage1(d_ref, kron_ref, bd_scr)
    first_row = lax.broadcasted_iota(jnp.int32, (half, 1), 0) == 0

    def freq(k, carry):
        n2, c = be_scr.shape[1:]
        be = be_scr[pl.ds(2 * k, 2)].reshape(2 * n2, c)
        bd = bd_scr[pl.ds(2 * k, 2)].reshape(2 * n2, c)
        re = _dot(g_ref[k, :half, :], be)
        im = _dot(g_ref[k, half:, :], bd)
        nyq = _dot(g_ref[k, half:half + ROW_GROUP, :], be)[0:1]
        is_dc = first_row & (k == 0)
        ka = re * jnp.where(is_dc, 0.5 / seq, 1.0 / seq)
        ka_ref[k] = ka
        kb_ref[k] = jnp.where(is_dc, 0.0, im * (1.0 / seq))
        ka2_ref[k] = jnp.where(is_dc, nyq * (0.5 / seq), ka)
        return carry

    lax.fori_loop(0, ka_ref.shape[0], freq, 0, unroll=FFT_UNROLL)


def _spectrum(mats, e, d):
    l = e.shape[0]
    kk, n2 = mats["g"].shape[0], FFT_N2
    n1h = l // n2
    cw = LANES
    out = jax.ShapeDtypeStruct((kk, n2, D_HY), F32)
    src = pl.BlockSpec((n1h, n2, cw), lambda h: (0, 0, h))
    dst = pl.BlockSpec((kk, n2, cw), lambda h: (0, 0, h))
    view = lambda a: a.reshape(n1h, n2, D_HY)
    return pl.pallas_call(
        functools.partial(_spectrum_kernel, seq=l),
        out_shape=(out, out, out),
        grid=(D_HY // cw,),
        in_specs=[src, src, _resident(mats["kron"].shape), _resident(mats["g"].shape)],
        out_specs=(dst, dst, dst),
        scratch_shapes=[pltpu.VMEM((2 * kk, n2, cw), BF16), pltpu.VMEM((2 * kk, n2, cw), BF16)],
        compiler_params=_params(("arbitrary",), 56),
        name="hyena_spectrum",
    )(view(e), view(d), mats["kron"], mats["g"])


def _long_conv_kernel(p_ref, x0_ref, kron_ref, kron_t_ref, g_ref, gt_ref, ka_ref, kb_ref, ka2_ref, bias_ref,
                      o_ref, b_scr):
    n1h, n2, c = p_ref.shape[1:]
    half = g_ref.shape[1] // 2
    _fft_stage1(p_ref.at[0], kron_ref, b_scr)

    def freq(k, carry):
        u = _fft_stage2(g_ref, b_scr, k)
        ure, uim = u[:half], u[half:]
        ka, kb, ka2 = ka_ref[k], kb_ref[k], ka2_ref[k]
        y = jnp.concatenate([ure * ka - uim * kb, ure * kb + uim * ka2], axis=0).astype(BF16)
        b_scr[pl.ds(2 * k, 2)] = _dot(gt_ref[k], y).astype(BF16).reshape(2, n2, c)
        return carry

    lax.fori_loop(0, g_ref.shape[0], freq, 0, unroll=FREQ_UNROLL)
    bias = bias_ref[...]

    def group(g, carry):
        grp = _row_group(g)
        z = b_scr[:, grp, :].reshape(b_scr.shape[0] * ROW_GROUP, c)
        y = _dot(kron_t_ref[...], z).reshape(n1h, ROW_GROUP, c)
        gated = (y + p_ref[0, :, grp, :].astype(F32) * bias) * x0_ref[0, :, grp, :].astype(F32)
        o_ref[0, :, grp, :] = gated.astype(BF16)
        return carry

    lax.fori_loop(0, n2 // ROW_GROUP, group, 0, unroll=FFT_UNROLL)


def _long_conv(p, x0, mats, ka, kb, ka2, bias):
    b, l, _ = p.shape
    kk, n2 = mats["g"].shape[0], FFT_N2
    n1h = l // n2
    cw = FFT_LANES
    view = lambda a: a.reshape(b, n1h, n2, D_HY)
    tile = pl.BlockSpec((1, n1h, n2, cw), lambda h, i: (i, 0, 0, h))
    spec = pl.BlockSpec((kk, n2, cw), lambda h, i: (0, 0, h), pipeline_mode=pl.Buffered(1))
    out = pl.pallas_call(
        _long_conv_kernel,
        out_shape=jax.ShapeDtypeStruct((b, n1h, n2, D_HY), BF16),
        grid=(D_HY // cw, b),
        in_specs=[
            tile, tile,
            _resident(mats["kron"].shape), _resident(mats["kron_t"].shape),
            _resident(mats["g"].shape), _resident(mats["g_t"].shape),
            spec, spec, spec,
            pl.BlockSpec((1, cw), lambda h, i: (0, h)),
        ],
        out_specs=tile,
        scratch_shapes=[pltpu.VMEM((2 * kk, n2, cw), BF16)],
        compiler_params=_params(("arbitrary", "arbitrary"), 60),
        name="hyena_long_conv",
    )(view(p), view(x0), mats["kron"], mats["kron_t"], mats["g"], mats["g_t"], ka, kb, ka2, bias.reshape(1, -1))
    return out.reshape(b, l, D_HY)


def _chunk_ffn_weights(wi, wo):
    nck = D_FF // FF_CHUNK
    g = wi[:, :D_FF].reshape(D_MODEL, nck, FF_CHUNK)
    u = wi[:, D_FF:].reshape(D_MODEL, nck, FF_CHUNK)
    wi_c = jnp.concatenate([g, u], axis=2).transpose(1, 0, 2).astype(BF16)
    return wi_c, wo.astype(BF16)


def _trunk(x, mod, weights, p):
    l = x.shape[1]
    depth = mod.shape[0]
    rope = _rope_tables(l)
    feats = _filter_features(l)
    mats = _fft_matrices(l)
    vec = lambda a: a.reshape(1, -1)
    for i in range(depth):
        w = weights[i]
        m = mod[i]
        x = _ffn(x, m, w["wi1"], w["wo1"], vec(p["ln_g"][i, 0]), vec(p["ln_b"][i, 0]))
        q, kd, vd, gated, x0 = _in_proj(x, m, w["w_in"], p["hy_conv_w"][i], p["hy_conv_b"][i], rope)
        o_attn = _attention(q, kd, vd, p["sink"][i])
        e, d = _filters(feats, p["hy_w1"][i], p["hy_b1"][i], p["hy_w2"][i], p["hy_b2"][i],
                        p["hy_w3"][i], p["hy_b3"][i], p["hy_w4"][i], p["hy_freq"][i], p["hy_decay"][i])
        ka, kb, ka2 = _spectrum(mats, e, d)
        o_hy = _long_conv(gated, x0, mats, ka, kb, ka2, p["hy_bias"][i])
        mix = (o_attn, o_hy, w["w_out"], vec(p["grp_norm_g"][i]), vec(p["ln_g"][i, 1]), vec(p["ln_b"][i, 1]))
        x = _ffn(x, m, w["wi2"], w["wo2"], vec(p["ln_g"][i, 2]), vec(p["ln_b"][i, 2]), mix)
    return x


def kernel(x_prompt, x_sample, c_prompt, c_sample, ada_w, ada_b, ffn1_wi, ffn1_wo, ffn2_wi, ffn2_wo, ln_g, ln_b, w_in, w_out, sink, grp_norm_g, hy_conv_w, hy_conv_b, hy_w1, hy_b1, hy_w2, hy_b2, hy_w3, hy_b3, hy_w4, hy_freq, hy_decay, hy_bias):
    p = dict(ln_g=ln_g, ln_b=ln_b, sink=sink, grp_norm_g=grp_norm_g, hy_conv_w=hy_conv_w,
             hy_conv_b=hy_conv_b, hy_w1=hy_w1, hy_b1=hy_b1, hy_w2=hy_w2, hy_b2=hy_b2, hy_w3=hy_w3,
             hy_b3=hy_b3, hy_w4=hy_w4, hy_freq=hy_freq, hy_decay=hy_decay, hy_bias=hy_bias)
    depth = ada_w.shape[0]
    weights = []
    for i in range(depth):
        wi1, wo1 = _chunk_ffn_weights(ffn1_wi[i], ffn1_wo[i])
        wi2, wo2 = _chunk_ffn_weights(ffn2_wi[i], ffn2_wo[i])
        weights.append(dict(wi1=wi1, wo1=wo1, wi2=wi2, wo2=wo2,
                            w_in=w_in[i].astype(BF16), w_out=w_out[i].astype(BF16)))
    nb = c_prompt.shape[0]
    mod = _modulation(jnp.concatenate([c_prompt, c_sample], axis=0), ada_w, ada_b)
    y_prompt = _trunk(x_prompt, mod[:, :nb], weights, p)
    y_sample = _trunk(x_sample, mod[:, nb:], weights, p)
    return (y_prompt, y_sample)
```
